```python
import math
import jax, jax.numpy as jnp
from jax import lax
import numpy as np

D_MODEL = 4096
BATCH = 8
SEQ = 2048
DEPTH = 2
DEC_BATCH = 4
DEC_SEQ = 4096
PAST_LEN = 128

MIX_WIDTH = D_MODEL
GROUP_WIDTH = MIX_WIDTH // 4
HEAD_DIM = 128
DA_HEADS = GROUP_WIDTH // HEAD_DIM
DA_PATTERNS = ((128, 1), (512, 4), (2048, 16))
DA_QBLK = 64
ROPE_THETA = 500000.0
ROPE_DIMS = HEAD_DIM // 4
GQA_HEADS = GROUP_WIDTH // HEAD_DIM
GQA_KV_HEADS = GQA_HEADS // 4
GQA_QBLK = 128
AXIAL_THETA = 10000.0
GRID_W = 64
HY_CH = GROUP_WIDTH
HY_ORDER = 2
HY_SHORT = 3
HY_EMB = 33
HY_BANDS = (HY_EMB - 1) // 2
HY_FILTER_W = 64
HY_MIN_DECAY = 3.07
HY_MAX_DECAY = 15.35
ML_HEADS = 4
ML_HEAD_DIM = GROUP_WIDTH // ML_HEADS
ML_CHUNK = 64
FF_DIM = -(-8 * D_MODEL // (3 * 256)) * 256
EPS = 1e-6
IN_SPLIT_SIZES = (GROUP_WIDTH, GROUP_WIDTH, GROUP_WIDTH,
                  GQA_HEADS * HEAD_DIM, GQA_KV_HEADS * HEAD_DIM, GQA_KV_HEADS * HEAD_DIM,
                  (HY_ORDER + 1) * HY_CH,
                  GROUP_WIDTH, GROUP_WIDTH, GROUP_WIDTH, GROUP_WIDTH, 4 * ML_HEADS)
N_IN = sum(IN_SPLIT_SIZES)

kernel_name = "hybrid_bidir_dilated_gqa_hyena_mlstm_encoder"

F32 = jnp.float32


def rmsnorm(x, g):
    xf = x.astype(F32)
    y = xf * lax.rsqrt(jnp.mean(xf * xf, axis=-1, keepdims=True) + EPS)
    return (y * g.astype(F32)).astype(x.dtype)


def rope_tables(pos, dims, theta):
    inv = jnp.float32(theta) ** (-jnp.arange(0, dims, 2, dtype=F32) / dims)
    ang = pos[:, None] * inv[None, :]
    return jnp.cos(ang), jnp.sin(ang)


def apply_rope(x, cos, sin):
    xf = x.astype(F32)
    x1, x2 = jnp.split(xf, 2, axis=-1)
    c = cos[:, None, :]
    s = sin[:, None, :]
    return jnp.concatenate([x1 * c - x2 * s, x2 * c + x1 * s], axis=-1).astype(x.dtype)


def partial_rope(x, cos, sin):
    return jnp.concatenate([apply_rope(x[..., :ROPE_DIMS], cos, sin), x[..., ROPE_DIMS:]], axis=-1)


def axial_rope(x, cos_r, sin_r, cos_c, sin_c):
    half = HEAD_DIM // 2
    return jnp.concatenate([apply_rope(x[..., :half], cos_r, sin_r),
                            apply_rope(x[..., half:], cos_c, sin_c)], axis=-1)


def dilated_window_attention(q, k, v):
    B, L, H, Dh = q.shape
    nblk = L // DA_QBLK
    scale = Dh ** -0.5
    q_blocks = jnp.moveaxis(q.reshape(B, nblk, DA_QBLK, H, Dh), 1, 0)

    def one_block(args):
        qb, bi = args
        tq = bi * DA_QBLK + jnp.arange(DA_QBLK)
        ms, dens, nums = [], [], []
        for window, dil in DA_PATTERNS:
            half = window // dil // 2
            offs = jnp.arange(-half, half + 1) * dil
            idx = tq[:, None] + offs[None, :]
            valid = (idx >= 0) & (idx < L)
            idx = jnp.clip(idx, 0, L - 1)
            kg = k[:, idx]
            vg = v[:, idx]
            s = jnp.einsum('bqhd,bqkhd->bhqk', qb, kg, preferred_element_type=F32) * scale
            s = jnp.where(valid[None, None], s, -jnp.inf)
            m = jnp.max(s, axis=-1)
            p = jnp.exp(s - m[..., None])
            ms.append(m)
            dens.append(jnp.sum(p, axis=-1))
            nums.append(jnp.einsum('bhqk,bqkhd->bhqd', p.astype(v.dtype), vg,
                                   preferred_element_type=F32))
        m_all = jnp.stack(ms)
        w = jnp.exp(m_all - jnp.max(m_all, axis=0))
        den = jnp.sum(w * jnp.stack(dens), axis=0)
        num = jnp.sum(w[..., None] * jnp.stack(nums), axis=0)
        return (num / den[..., None]).astype(q.dtype)

    out = lax.map(one_block, (q_blocks, jnp.arange(nblk)))
    return out.transpose(1, 0, 3, 2, 4).reshape(B, L, H * Dh)


def gqa_attention(q, k, v):
    B, L, Hq, Dh = q.shape
    Hkv = k.shape[2]
    G = Hq // Hkv
    nblk = L // GQA_QBLK
    scale = Dh ** -0.5
    q_blocks = jnp.moveaxis(q.reshape(B, nblk, GQA_QBLK, Hkv, G, Dh), 1, 0)

    def one_block(qb):
        s = jnp.einsum('bqhgd,bkhd->bhgqk', qb, k, preferred_element_type=F32) * scale
        p = jax.nn.softmax(s, axis=-1)
        return jnp.einsum('bhgqk,bkhd->bqhgd', p.astype(v.dtype), v).astype(q.dtype)

    out = lax.map(one_block, q_blocks)
    return jnp.moveaxis(out, 0, 1).reshape(B, L, Hq * Dh)


def short_conv(u, w, b):
    pad = HY_SHORT // 2
    L = u.shape[1]
    up = jnp.pad(u, ((0, 0), (pad, pad), (0, 0)))
    y = b
    for j in range(HY_SHORT):
        y = y + up[:, j:j + L] * w[j]
    return y


def hyena_filters(L, w1, b1, fr1, w2, b2, fr2, w3, decay):
    n = jnp.arange(L, dtype=F32)
    t = n / (L - 1)
    f = jnp.linspace(1e-4, HY_BANDS - 1, HY_BANDS, dtype=F32)
    ang = (2.0 * math.pi / L) * n[:, None] * f[None, :]
    z = jnp.concatenate([t[:, None], jnp.cos(ang), -jnp.sin(ang)], axis=-1)
    h = jnp.sin(fr1.astype(F32) * (z @ w1.astype(F32) + b1.astype(F32)))
    h = jnp.sin(fr2.astype(F32) * (h @ w2.astype(F32) + b2.astype(F32)))
    h = h @ w3.astype(F32)
    r = jnp.abs(n - L // 2) / (L // 2)
    h = h * jnp.exp(-r[:, None] * decay.astype(F32)[None, :])
    return h.reshape(L, HY_ORDER, HY_CH)


def centred_fftconv(u, h):
    L = u.shape[1]
    U = jnp.fft.rfft(u, n=2 * L, axis=1)
    Hf = jnp.fft.rfft(h, n=2 * L, axis=0)
    y = jnp.fft.irfft(U * Hf[None], n=2 * L, axis=1)
    return y[:, L // 2:L // 2 + L]


def hyena_mixer(u, conv_w, conv_b, w1, b1, fr1, w2, b2, fr2, w3, decay, skip):
    L = u.shape[1]
    uc = short_conv(u, conv_w, conv_b).astype(F32)
    v, x1, x2 = jnp.split(uc, 3, axis=-1)
    h = hyena_filters(L, w1, b1, fr1, w2, b2, fr2, w3, decay)
    sk = skip.astype(F32).reshape(HY_ORDER, HY_CH)
    z = v
    for o, gate in enumerate((x1, x2)):
        z = gate * (centred_fftconv(z, h[:, o]) + sk[o] * z)
    return z.astype(u.dtype)


def mlstm_chunkwise(q, k, v, log_i, log_f):
    B, H, L, d = q.shape
    nc = L // ML_CHUNK

    def chunks(a):
        return jnp.moveaxis(a.reshape((B, H, nc, ML_CHUNK) + a.shape[3:]), 2, 0)

    causal = jnp.tril(jnp.ones((ML_CHUNK, ML_CHUNK), dtype=bool))

    def step(carry, xs):
        C, n, m = carry
        qc, kc, vc, li, lf = xs
        b = jnp.cumsum(lf, axis=-1)
        Dm = b[..., :, None] - b[..., None, :] + li[..., None, :]
        Dm = jnp.where(causal, Dm, -jnp.inf)
        inter = b + m[..., None]
        m_t = jnp.maximum(inter, jnp.max(Dm, axis=-1))
        Wm = jnp.exp(Dm - m_t[..., None])
        a_inter = jnp.exp(inter - m_t)
        S = jnp.einsum('bhtd,bhsd->bhts', qc, kc) * Wm
        num = a_inter[..., None] * jnp.einsum('bhtd,bhde->bhte', qc, C) + jnp.einsum('bhts,bhse->bhte', S, vc)
        den = a_inter * jnp.einsum('bhtd,bhd->bht', qc, n) + jnp.sum(S, axis=-1)
        h = num / jnp.maximum(jnp.abs(den), jnp.exp(-m_t))[..., None]
        bL = b[..., -1]
        g = bL[..., None] - b + li
        m_new = jnp.maximum(bL + m, jnp.max(g, axis=-1))
        a_old = jnp.exp(bL + m - m_new)
        wk = jnp.exp(g - m_new[..., None])
        C_new = a_old[..., None, None] * C + jnp.einsum('bhsd,bhse->bhde', kc * wk[..., None], vc)
        n_new = a_old[..., None] * n + jnp.einsum('bhs,bhsd->bhd', wk, kc)
        return (C_new, n_new, m_new), h

    init = (jnp.zeros((B, H, d, d), F32), jnp.zeros((B, H, d), F32), jnp.zeros((B, H), F32))
    _, hs = lax.scan(step, init, (chunks(q), chunks(k), chunks(v), chunks(log_i), chunks(log_f)))
    return jnp.moveaxis(hs, 0, 2).reshape(B, H, L, d)


def mlstm_mixer(q, k, v, o, gates, gate_b, norm_g):
    B, L, _ = q.shape

    def heads(a):
        return a.reshape(B, L, ML_HEADS, ML_HEAD_DIM).transpose(0, 2, 1, 3).astype(F32)

    qh = heads(q)
    kh = heads(k) * (ML_HEAD_DIM ** -0.5)
    vh = heads(v)
    g = (gates.astype(F32).reshape(B, L, 4, ML_HEADS) + gate_b.astype(F32)).transpose(2, 0, 3, 1)
    h_fwd = mlstm_chunkwise(qh, kh, vh, g[0], jax.nn.log_sigmoid(g[1]))
    fl = lambda a: jnp.flip(a, axis=2)
    h_bwd = fl(mlstm_chunkwise(fl(qh), fl(kh), fl(vh), fl(g[2]), fl(jax.nn.log_sigmoid(g[3]))))
    hs = h_fwd + h_bwd
    hs = hs * lax.rsqrt(jnp.mean(hs * hs, axis=-1, keepdims=True) + EPS)
    hs = hs.transpose(0, 2, 1, 3).reshape(B, L, GROUP_WIDTH) * norm_g.astype(F32)
    return (hs * jax.nn.sigmoid(o.astype(F32))).astype(q.dtype)


def trunk(x, norm1_g, w_in, qk_norm_g, hy_conv_w, hy_conv_b, hy_w1, hy_b1, hy_freq1,
          hy_w2, hy_b2, hy_freq2, hy_w3, hy_decay, hy_skip, ml_gate_b, ml_norm_g,
          w_out, norm2_g, w_gate, w_up, w_down, final_g):
    B, L, _ = x.shape
    rows = L // GRID_W
    cos_t, sin_t = rope_tables(jnp.arange(L, dtype=F32), ROPE_DIMS, ROPE_THETA)
    row_pos = jnp.repeat(jnp.arange(rows, dtype=F32), GRID_W)
    col_pos = jnp.tile(jnp.arange(GRID_W, dtype=F32), rows)
    cos_r, sin_r = rope_tables(row_pos, HEAD_DIM // 2, AXIAL_THETA)
    cos_c, sin_c = rope_tables(col_pos, HEAD_DIM // 2, AXIAL_THETA)
    bounds = np.cumsum(IN_SPLIT_SIZES)[:-1].tolist()
    for l in range(DEPTH):
        h = rmsnorm(x, norm1_g[l])
        proj = h @ w_in[l]
        (a_q, a_k, a_v, b_q, b_k, b_v, c_u, d_q, d_k, d_v, d_o, d_g) = jnp.split(proj, bounds, axis=-1)
        a_q = partial_rope(a_q.reshape(B, L, DA_HEADS, HEAD_DIM), cos_t, sin_t)
        a_k = partial_rope(a_k.reshape(B, L, DA_HEADS, HEAD_DIM), cos_t, sin_t)
        y_a = dilated_window_attention(a_q, a_k, a_v.reshape(B, L, DA_HEADS, HEAD_DIM))
        b_q = axial_rope(rmsnorm(b_q.reshape(B, L, GQA_HEADS, HEAD_DIM), qk_norm_g[l, 0]), cos_r, sin_r, cos_c, sin_c)
        b_k = axial_rope(rmsnorm(b_k.reshape(B, L, GQA_KV_HEADS, HEAD_DIM), qk_norm_g[l, 1]), cos_r, sin_r, cos_c, sin_c)
        y_b = gqa_attention(b_q, b_k, b_v.reshape(B, L, GQA_KV_HEADS, HEAD_DIM))
        y_c = hyena_mixer(c_u, hy_conv_w[l], hy_conv_b[l], hy_w1[l], hy_b1[l], hy_freq1[l],
                          hy_w2[l], hy_b2[l], hy_freq2[l], hy_w3[l], hy_decay[l], hy_skip[l])
        y_d = mlstm_mixer(d_q, d_k, d_v, d_o, d_g, ml_gate_b[l], ml_norm_g[l])
        x = x + jnp.concatenate([y_a, y_b, y_c, y_d], axis=-1) @ w_out[l]
        h = rmsnorm(x, norm2_g[l])
        x = x + (jax.nn.silu(h @ w_gate[l]) * (h @ w_up[l])) @ w_down[l]
    return rmsnorm(x, final_g)


def setup_inputs(seed: int = 0) -> dict:
    key = jax.random.key(seed)
    ks = jax.random.split(key, 32)

    def nrm(k, shape, std):
        return std * jax.random.normal(k, shape, F32)

    x_prompt = jax.random.normal(ks[0], (BATCH, SEQ, D_MODEL), F32)
    x_sample = jax.random.normal(ks[1], (DEC_BATCH, DEC_SEQ, D_MODEL), F32)
    norm1_g = 1.0 + nrm(ks[2], (DEPTH, D_MODEL), 0.02)
    w_in = nrm(ks[3], (DEPTH, D_MODEL, N_IN), D_MODEL ** -0.5)
    qk_norm_g = 1.0 + nrm(ks[4], (DEPTH, 2, HEAD_DIM), 0.02)
    hy_conv_w = nrm(ks[5], (DEPTH, HY_SHORT, (HY_ORDER + 1) * HY_CH), HY_SHORT ** -0.5)
    hy_conv_b = nrm(ks[6], (DEPTH, (HY_ORDER + 1) * HY_CH), 0.02)
    hy_w1 = nrm(ks[7], (DEPTH, HY_EMB, HY_FILTER_W), HY_EMB ** -0.5)
    hy_b1 = nrm(ks[8], (DEPTH, HY_FILTER_W), 0.02)
    hy_freq1 = 1.0 + nrm(ks[9], (DEPTH, HY_FILTER_W), 0.1)
    hy_w2 = nrm(ks[10], (DEPTH, HY_FILTER_W, HY_FILTER_W), HY_FILTER_W ** -0.5)
    hy_b2 = nrm(ks[11], (DEPTH, HY_FILTER_W), 0.02)
    hy_freq2 = 1.0 + nrm(ks[12], (DEPTH, HY_FILTER_W), 0.1)
    hy_w3 = nrm(ks[13], (DEPTH, HY_FILTER_W, HY_ORDER * HY_CH), 0.01)
    hy_decay = jax.random.uniform(ks[14], (DEPTH, HY_ORDER * HY_CH), F32, HY_MIN_DECAY, HY_MAX_DECAY)
    hy_skip = nrm(ks[15], (DEPTH, HY_ORDER * HY_CH), 0.1)
    i_bias = nrm(ks[16], (DEPTH, 2, ML_HEADS), 0.1)
    f_bias = jnp.linspace(3.0, 6.0, ML_HEADS, dtype=F32) + nrm(ks[17], (DEPTH, 2, ML_HEADS), 0.1)
    ml_gate_b = jnp.stack([i_bias[:, 0], f_bias[:, 0], i_bias[:, 1], f_bias[:, 1]], axis=1)
    ml_norm_g = 1.0 + nrm(ks[18], (DEPTH, GROUP_WIDTH), 0.02)
    w_out = nrm(ks[19], (DEPTH, MIX_WIDTH, D_MODEL), MIX_WIDTH ** -0.5)
    norm2_g = 1.0 + nrm(ks[20], (DEPTH, D_MODEL), 0.02)
    w_gate = nrm(ks[21], (DEPTH, D_MODEL, FF_DIM), D_MODEL ** -0.5)
    w_up = nrm(ks[22], (DEPTH, D_MODEL, FF_DIM), D_MODEL ** -0.5)
    w_down = nrm(ks[23], (DEPTH, FF_DIM, D_MODEL), FF_DIM ** -0.5)
    final_g = 1.0 + nrm(ks[24], (D_MODEL,), 0.02)
    return {"x_prompt": x_prompt, "x_sample": x_sample, "norm1_g": norm1_g, "w_in": w_in,
            "qk_norm_g": qk_norm_g, "hy_conv_w": hy_conv_w, "hy_conv_b": hy_conv_b,
            "hy_w1": hy_w1, "hy_b1": hy_b1, "hy_freq1": hy_freq1, "hy_w2": hy_w2,
            "hy_b2": hy_b2, "hy_freq2": hy_freq2, "hy_w3": hy_w3, "hy_decay": hy_decay,
            "hy_skip": hy_skip, "ml_gate_b": ml_gate_b, "ml_norm_g": ml_norm_g,
            "w_out": w_out, "norm2_g": norm2_g, "w_gate": w_gate, "w_up": w_up,
            "w_down": w_down, "final_g": final_g}


def reference(x_prompt, x_sample, norm1_g, w_in, qk_norm_g, hy_conv_w, hy_conv_b, hy_w1,
              hy_b1, hy_freq1, hy_w2, hy_b2, hy_freq2, hy_w3, hy_decay, hy_skip, ml_gate_b,
              ml_norm_g, w_out, norm2_g, w_gate, w_up, w_down, final_g):
    weights = (norm1_g, w_in, qk_norm_g, hy_conv_w, hy_conv_b, hy_w1, hy_b1, hy_freq1,
               hy_w2, hy_b2, hy_freq2, hy_w3, hy_decay, hy_skip, ml_gate_b, ml_norm_g,
               w_out, norm2_g, w_gate, w_up, w_down, final_g)
    y_prompt = trunk(x_prompt, *weights)
    y_sample = trunk(x_sample, *weights)
    return (y_prompt, y_sample)
```

```python
import functools
import math

import jax
import jax.numpy as jnp
from jax import lax
from jax.experimental import pallas as pl
from jax.experimental.pallas import tpu as pltpu

F32 = jnp.float32
BF16 = jnp.bfloat16

D_MODEL = 4096
DEPTH = 2
GROUP_WIDTH = 1024
HEAD_DIM = 128
DA_HEADS = 8
DA_PATTERNS = ((128, 1), (512, 4), (2048, 16))
ROPE_THETA = 500000.0
ROPE_DIMS = 32
GQA_HEADS = 8
GQA_KV_HEADS = 2
GQA_GROUP = GQA_HEADS // GQA_KV_HEADS
AXIAL_THETA = 10000.0
GRID_W = 64
HY_CH = 1024
HY_ORDER = 2
HY_EMB = 33
HY_BANDS = 16
ML_HEADS = 4
ML_HEAD_DIM = 256
FF_DIM = 11008
EPS = 1e-6
N_MAIN = 11776
N_GATE = 4 * ML_HEADS

OFF_AQ, OFF_AK, OFF_AV = 0, 1024, 2048
OFF_BQ, OFF_BK, OFF_BV = 3072, 4096, 4352
OFF_CU = 4608
OFF_DQ, OFF_DK, OFF_DV, OFF_DO = 7680, 8704, 9728, 10752

LANES = 128
NEG = -1e30
VMEM_LIMIT = 56 * 1024 * 1024


def _cparams(*sem):
    return pltpu.CompilerParams(dimension_semantics=sem, vmem_limit_bytes=VMEM_LIMIT)


def _rmsnorm_kernel(x_ref, g_ref, o_ref):
    x = x_ref[...].astype(F32)
    ms = jnp.mean(x * x, axis=-1, keepdims=True)
    o_ref[...] = (x * lax.rsqrt(ms + EPS) * g_ref[...]).astype(o_ref.dtype)


def rmsnorm(x, g, out_dtype, tm=256):
    m, d = x.shape
    return pl.pallas_call(
        _rmsnorm_kernel,
        grid=(m // tm,),
        in_specs=[pl.BlockSpec((tm, d), lambda i: (i, 0)),
                  pl.BlockSpec((1, d), lambda i: (0, 0))],
        out_specs=pl.BlockSpec((tm, d), lambda i: (i, 0)),
        out_shape=jax.ShapeDtypeStruct((m, d), out_dtype),
        compiler_params=_cparams("parallel"),
        name="rmsnorm",
    )(x, g.reshape(1, d).astype(F32))


def _linear_kernel(x_ref, w_ref, o_ref):
    o_ref[...] = jnp.dot(x_ref[...], w_ref[...], preferred_element_type=F32).astype(o_ref.dtype)


def _linear_res_kernel(x_ref, w_ref, r_ref, o_ref):
    acc = jnp.dot(x_ref[...], w_ref[...], preferred_element_type=F32)
    o_ref[...] = (r_ref[...] + acc).astype(o_ref.dtype)


def linear(x, w, out_dtype, tm, tn, residual=None, name="linear"):
    m, k = x.shape
    n = w.shape[1]
    in_specs = [pl.BlockSpec((tm, k), lambda i, j: (i, 0)),
                pl.BlockSpec((k, tn), lambda i, j: (0, j))]
    args = [x, w]
    body = _linear_kernel
    if residual is not None:
        in_specs.append(pl.BlockSpec((tm, tn), lambda i, j: (i, j)))
        args.append(residual)
        body = _linear_res_kernel
    return pl.pallas_call(
        body,
        grid=(m // tm, n // tn),
        in_specs=in_specs,
        out_specs=pl.BlockSpec((tm, tn), lambda i, j: (i, j)),
        out_shape=jax.ShapeDtypeStruct((m, n), out_dtype),
        compiler_params=_cparams("parallel", "arbitrary"),
        name=name,
    )(*args)


def _ffn_up_kernel(h_ref, wg_ref, wu_ref, o_ref):
    h = h_ref[...]
    g = jnp.dot(h, wg_ref[...], preferred_element_type=F32)
    u = jnp.dot(h, wu_ref[...], preferred_element_type=F32)
    o_ref[...] = (g * jax.nn.sigmoid(g) * u).astype(o_ref.dtype)


def ffn_up(h, wg, wu, tm, tn):
    m, k = h.shape
    n = wg.shape[1]
    return pl.pallas_call(
        _ffn_up_kernel,
        grid=(m // tm, n // tn),
        in_specs=[pl.BlockSpec((tm, k), lambda i, j: (i, 0)),
                  pl.BlockSpec((k, tn), lambda i, j: (0, j)),
                  pl.BlockSpec((k, tn), lambda i, j: (0, j))],
        out_specs=pl.BlockSpec((tm, tn), lambda i, j: (i, j)),
        out_shape=jax.ShapeDtypeStruct((m, n), BF16),
        compiler_params=_cparams("parallel", "arbitrary"),
        name="ffn_up",
    )(h, wg, wu)


def _rope_kernel(x_ref, c_ref, s_ref, g_ref, o_ref, *, heads, half, norm, scale):
    c = c_ref[...]
    s = s_ref[...]
    lane = lax.broadcasted_iota(jnp.int32, c.shape, 1)
    first = (lane % (2 * half)) < half
    for h in range(heads):
        sl = slice(h * HEAD_DIM, (h + 1) * HEAD_DIM)
        x = x_ref[:, sl].astype(F32)
        if norm:
            ms = jnp.mean(x * x, axis=-1, keepdims=True)
            x = x * lax.rsqrt(ms + EPS) * g_ref[...]
        partner = jnp.where(first, pltpu.roll(x, HEAD_DIM - half, 1), pltpu.roll(x, half, 1))
        o_ref[:, sl] = ((x * c + partner * s) * scale).astype(o_ref.dtype)


def rope_prep(proj, col_off, heads, cos_t, sin_t, g, seq_len, *, half, norm, scale, tm=512):
    m = proj.shape[0]
    width = heads * HEAD_DIM
    nblk = seq_len // tm
    kern = functools.partial(_rope_kernel, heads=heads, half=half, norm=norm, scale=scale)
    return pl.pallas_call(
        kern,
        grid=(m // tm,),
        in_specs=[pl.BlockSpec((tm, width), lambda i: (i, col_off // width)),
                  pl.BlockSpec((tm, HEAD_DIM), lambda i: (i % nblk, 0)),
                  pl.BlockSpec((tm, HEAD_DIM), lambda i: (i % nblk, 0)),
                  pl.BlockSpec((1, HEAD_DIM), lambda i: (0, 0))],
        out_specs=pl.BlockSpec((tm, width), lambda i: (i, 0)),
        out_shape=jax.ShapeDtypeStruct((m, width), BF16),
        compiler_params=_cparams("parallel"),
        name="rope_prep",
    )(proj, cos_t, sin_t, g.reshape(1, HEAD_DIM).astype(F32))


def _lane_tables(cos, sin, reps_after):
    c = jnp.concatenate([cos, cos], axis=-1)
    s = jnp.concatenate([-sin, sin], axis=-1)
    if reps_after:
        c = jnp.concatenate([c, jnp.ones((c.shape[0], reps_after), F32)], axis=-1)
        s = jnp.concatenate([s, jnp.zeros((s.shape[0], reps_after), F32)], axis=-1)
    return c, s


def _rope_angles(pos, dims, theta):
    inv = jnp.float32(theta) ** (-jnp.arange(0, dims, 2, dtype=F32) / dims)
    ang = pos[:, None] * inv[None, :]
    return jnp.cos(ang), jnp.sin(ang)


def rotary_tables(seq_len):
    cos_t, sin_t = _rope_angles(jnp.arange(seq_len, dtype=F32), ROPE_DIMS, ROPE_THETA)
    ca, sa = _lane_tables(cos_t, sin_t, HEAD_DIM - ROPE_DIMS)
    rows = seq_len // GRID_W
    row_pos = jnp.repeat(jnp.arange(rows, dtype=F32), GRID_W)
    col_pos = jnp.tile(jnp.arange(GRID_W, dtype=F32), rows)
    cr, sr = _lane_tables(*_rope_angles(row_pos, HEAD_DIM // 2, AXIAL_THETA), 0)
    cc, sc = _lane_tables(*_rope_angles(col_pos, HEAD_DIM // 2, AXIAL_THETA), 0)
    cb = jnp.concatenate([cr, cc], axis=-1)
    sb = jnp.concatenate([sr, sc], axis=-1)
    return ca, sa, cb, sb


DA_QBLK = 128
DA_HALF = 64


def _dilated_kernel(q_ref, k_ref, v_ref, o_ref, lse_ref, *, la, win):
    nq = la // DA_QBLK

    def body(qi, carry):
        a0 = pl.multiple_of(qi * DA_QBLK, DA_QBLK)
        ws = pl.multiple_of(jnp.clip(a0 - DA_HALF, 0, la - win), DA_HALF)
        q = q_ref[0, pl.ds(a0, DA_QBLK), :]
        kw = k_ref[0, pl.ds(ws, win), :]
        vw = v_ref[0, pl.ds(ws, win), :]
        s = lax.dot_general(q, kw, (((1,), (1,)), ((), ())), preferred_element_type=F32)
        qpos = a0 + lax.broadcasted_iota(jnp.int32, s.shape, 0)
        kpos = ws + lax.broadcasted_iota(jnp.int32, s.shape, 1)
        s = jnp.where(jnp.abs(kpos - qpos) <= DA_HALF, s, NEG)
        m = jnp.max(s, axis=-1, keepdims=True)
        p = jnp.exp(s - m)
        l = jnp.sum(p, axis=-1, keepdims=True)
        o = jnp.dot(p.astype(BF16), vw, preferred_element_type=F32)
        o_ref[0, pl.ds(a0, DA_QBLK), :] = o / l
        lse_ref[0, pl.ds(a0, DA_QBLK), :] = jnp.broadcast_to(m + jnp.log(l), (DA_QBLK, HEAD_DIM))
        return carry

    lax.fori_loop(0, nq, body, 0)


def dilated_pattern(q, k, proj, batch, seq_len, dil):
    la = seq_len // dil
    win = min(2 * DA_QBLK, la)
    n_proj = proj.shape[-1]
    qv = q.reshape(batch, la, dil * GROUP_WIDTH)
    kv = k.reshape(batch, la, dil * GROUP_WIDTH)
    pv = proj.reshape(batch, la, dil * n_proj)
    v_blk0 = OFF_AV // HEAD_DIM
    per_row = n_proj // HEAD_DIM
    blk = (1, la, HEAD_DIM)
    out = jax.ShapeDtypeStruct((batch, la, dil * GROUP_WIDTH), F32)
    o, lse = pl.pallas_call(
        functools.partial(_dilated_kernel, la=la, win=win),
        grid=(batch, dil * DA_HEADS),
        in_specs=[pl.BlockSpec(blk, lambda b, j: (b, 0, j)),
                  pl.BlockSpec(blk, lambda b, j: (b, 0, j)),
                  pl.BlockSpec(blk, lambda b, j: (b, 0, (j // DA_HEADS) * per_row + v_blk0 + j % DA_HEADS))],
        out_specs=[pl.BlockSpec(blk, lambda b, j: (b, 0, j)),
                   pl.BlockSpec(blk, lambda b, j: (b, 0, j))],
        out_shape=[out, out],
        compiler_params=_cparams("parallel", "parallel"),
        name=f"dilated_d{dil}",
    )(qv, kv, pv)
    m = batch * seq_len
    return o.reshape(m, GROUP_WIDTH), lse.reshape(m, GROUP_WIDTH)


def _merge_kernel(o1, o2, o3, l1, l2, l3, y_ref):
    a1, a2, a3 = l1[...], l2[...], l3[...]
    mx = jnp.maximum(jnp.maximum(a1, a2), a3)
    w1, w2, w3 = jnp.exp(a1 - mx), jnp.exp(a2 - mx), jnp.exp(a3 - mx)
    num = w1 * o1[...] + w2 * o2[...] + w3 * o3[...]
    y_ref[...] = (num / (w1 + w2 + w3)).astype(y_ref.dtype)


def merge_patterns(outs, lses, tm=256):
    m, w = outs[0].shape
    spec = pl.BlockSpec((tm, w), lambda i: (i, 0))
    return pl.pallas_call(
        _merge_kernel,
        grid=(m // tm,),
        in_specs=[spec] * 6,
        out_specs=spec,
        out_shape=jax.ShapeDtypeStruct((m, w), BF16),
        compiler_params=_cparams("parallel"),
        name="dilated_merge",
    )(*outs, *lses)


def _gqa_kernel(q_ref, k_ref, v_ref, o_ref):
    k = k_ref[0]
    v = v_ref[0]
    for g in range(GQA_GROUP):
        sl = slice(g * HEAD_DIM, (g + 1) * HEAD_DIM)
        q = q_ref[0, :, sl]
        s = lax.dot_general(q, k, (((1,), (1,)), ((), ())), preferred_element_type=F32)
        m = jnp.max(s, axis=-1, keepdims=True)
        p = jnp.exp(s - m)
        l = jnp.sum(p, axis=-1, keepdims=True)
        o = jnp.dot(p.astype(BF16), v, preferred_element_type=F32)
        o_ref[0, :, sl] = (o / l).astype(o_ref.dtype)


def gqa_attention(q, k, proj, batch, seq_len, tq=256):
    qv = q.reshape(batch, seq_len, GQA_HEADS * HEAD_DIM)
    kv = k.reshape(batch, seq_len, GQA_KV_HEADS * HEAD_DIM)
    pv = proj.reshape(batch, seq_len, proj.shape[-1])
    gw = GQA_GROUP * HEAD_DIM
    v_blk0 = OFF_BV // HEAD_DIM
    y = pl.pallas_call(
        _gqa_kernel,
        grid=(batch, GQA_KV_HEADS, seq_len // tq),
        in_specs=[pl.BlockSpec((1, tq, gw), lambda b, h, i: (b, i, h)),
                  pl.BlockSpec((1, seq_len, HEAD_DIM), lambda b, h, i: (b, 0, h)),
                  pl.BlockSpec((1, seq_len, HEAD_DIM), lambda b, h, i: (b, 0, v_blk0 + h))],
        out_specs=pl.BlockSpec((1, tq, gw), lambda b, h, i: (b, i, h)),
        out_shape=jax.ShapeDtypeStruct((batch, seq_len, GQA_HEADS * HEAD_DIM), BF16),
        compiler_params=_cparams("parallel", "parallel", "arbitrary"),
        name="gqa_attention",
    )(qv, kv, pv)
    return y.reshape(batch * seq_len, GQA_HEADS * HEAD_DIM)


HY_TC = 256
DFT_TM = 512
DFT_TN = 512


def _short_conv_kernel(u_ref, w_ref, b_ref, o_ref):
    u = u_ref[0].astype(F32)
    n = u.shape[0]
    row = lax.broadcasted_iota(jnp.int32, u.shape, 0)
    prev = jnp.where(row == 0, 0.0, pltpu.roll(u, 1, 0))
    nxt = jnp.where(row == n - 1, 0.0, pltpu.roll(u, n - 1, 0))
    o_ref[0] = b_ref[...] + prev * w_ref[0:1, :] + u * w_ref[1:2, :] + nxt * w_ref[2:3, :]


def short_conv(proj, conv_w, conv_b, batch, seq_len):
    pv = proj.reshape(batch, seq_len, proj.shape[-1])
    width = conv_w.shape[-1]
    blk0 = OFF_CU // HY_TC
    return pl.pallas_call(
        _short_conv_kernel,
        grid=(batch, width // HY_TC),
        in_specs=[pl.BlockSpec((1, seq_len, HY_TC), lambda b, j: (b, 0, blk0 + j)),
                  pl.BlockSpec((3, HY_TC), lambda b, j: (0, j)),
                  pl.BlockSpec((1, HY_TC), lambda b, j: (0, j))],
        out_specs=pl.BlockSpec((1, seq_len, HY_TC), lambda b, j: (b, 0, j)),
        out_shape=jax.ShapeDtypeStruct((batch, seq_len, width), F32),
        compiler_params=_cparams("parallel", "parallel"),
        name="hyena_short_conv",
    )(pv, conv_w.astype(F32), conv_b.reshape(1, width).astype(F32))


def dft_matrices(seq_len):
    n = seq_len
    half = DFT_TM // 2
    i = jnp.arange(2 * n, dtype=jnp.int32)
    tile, w = i // DFT_TM, i % DFT_TM
    imag = (w // half) == 1
    odd = 2 * (tile * half + w % half) + 1
    s = jnp.arange(n, dtype=jnp.int32)
    unit = math.pi / (2 * n)
    ang_f = ((odd[:, None] * s[None, :]) % (4 * n)).astype(F32) * unit
    fwd = jnp.where(imag[:, None], -jnp.sin(ang_f), jnp.cos(ang_f)).astype(BF16)
    t = s + n // 2
    ang_g = ((t[:, None] * odd[None, :]) % (4 * n)).astype(F32) * unit
    inv = (jnp.where(imag[None, :], -jnp.sin(ang_g), jnp.cos(ang_g)) / n).astype(BF16)
    return fwd, inv


def hyena_filters(seq_len, w1, b1, fr1, w2, b2, fr2, w3, decay):
    hp = lax.Precision.HIGHEST
    n = jnp.arange(seq_len, dtype=F32)
    t = n / (seq_len - 1)
    f = jnp.linspace(1e-4, HY_BANDS - 1, HY_BANDS, dtype=F32)
    ang = (2.0 * math.pi / seq_len) * n[:, None] * f[None, :]
    z = jnp.concatenate([t[:, None], jnp.cos(ang), -jnp.sin(ang)], axis=-1)
    h = jnp.sin(fr1 * (jnp.dot(z, w1, precision=hp) + b1))
    h = jnp.sin(fr2 * (jnp.dot(h, w2, precision=hp) + b2))
    h = jnp.dot(h, w3, precision=hp)
    r = jnp.abs(n - seq_len // 2) / (seq_len // 2)
    return h * jnp.exp(-r[:, None] * decay[None, :])


def _dft_fwd_kernel(f_ref, z_ref, h_ref, p_ref, zb_ref):
    @pl.when(pl.program_id(2) == 0)
    def _():
        zb_ref[...] = z_ref[0].astype(BF16)

    acc = jnp.dot(f_ref[...], zb_ref[...], preferred_element_type=F32)
    half = DFT_TM // 2
    ur, ui = acc[:half], acc[half:]
    hr, hi = h_ref[:half, :], h_ref[half:, :]
    p_ref[0, :half, :] = (ur * hr - ui * hi).astype(p_ref.dtype)
    p_ref[0, half:, :] = (ur * hi + ui * hr).astype(p_ref.dtype)


def dft_forward(fwd, z, z_blk0, spec, spec_blk0):
    batch, seq_len, _ = z.shape
    return pl.pallas_call(
        _dft_fwd_kernel,
        grid=(batch, HY_CH // DFT_TN, 2 * seq_len // DFT_TM),
        in_specs=[pl.BlockSpec((DFT_TM, seq_len), lambda b, j, i: (i, 0)),
                  pl.BlockSpec((1, seq_len, DFT_TN), lambda b, j, i: (b, 0, z_blk0 + j)),
                  pl.BlockSpec((DFT_TM, DFT_TN), lambda b, j, i: (i, spec_blk0 + j))],
        out_specs=pl.BlockSpec((1, DFT_TM, DFT_TN), lambda b, j, i: (b, i, j)),
        out_shape=jax.ShapeDtypeStruct((batch, 2 * seq_len, HY_CH), BF16),
        scratch_shapes=[pltpu.VMEM((seq_len, DFT_TN), BF16)],
        compiler_params=_cparams("parallel", "parallel", "arbitrary"),
        name="hyena_dft_fwd",
    )(fwd, z, spec)


def _dft_inv_kernel(g_ref, p_ref, z_ref, x_ref, sk_ref, o_ref):
    y = jnp.dot(g_ref[...], p_ref[0], preferred_element_type=F32)
    o_ref[0] = (x_ref[0] * (y + sk_ref[...] * z_ref[0])).astype(o_ref.dtype)


def dft_inverse(inv, p, z, z_blk0, gate, gate_blk0, skip, skip_blk0, out_dtype, tm=256):
    batch, seq2, _ = p.shape
    seq_len = seq2 // 2
    return pl.pallas_call(
        _dft_inv_kernel,
        grid=(batch, HY_CH // DFT_TN, seq_len // tm),
        in_specs=[pl.BlockSpec((tm, seq2), lambda b, j, i: (i, 0)),
                  pl.BlockSpec((1, seq2, DFT_TN), lambda b, j, i: (b, 0, j)),
                  pl.BlockSpec((1, tm, DFT_TN), lambda b, j, i: (b, i, z_blk0 + j)),
                  pl.BlockSpec((1, tm, DFT_TN), lambda b, j, i: (b, i, gate_blk0 + j)),
                  pl.BlockSpec((1, DFT_TN), lambda b, j, i: (0, skip_blk0 + j))],
        out_specs=pl.BlockSpec((1, tm, DFT_TN), lambda b, j, i: (b, i, j)),
        out_shape=jax.ShapeDtypeStruct((batch, seq_len, HY_CH), out_dtype),
        compiler_params=_cparams("parallel", "parallel", "arbitrary"),
        name="hyena_dft_inv",
    )(inv, p, z, gate, skip)


def hyena_mixer(proj, batch, seq_len, fwd, inv, spec, conv_w, conv_b, skip):
    uc = short_conv(proj, conv_w, conv_b, batch, seq_len)
    nb = HY_CH // DFT_TN
    sk = skip.reshape(1, HY_ORDER * HY_CH).astype(F32)
    p = dft_forward(fwd, uc, 0, spec, 0)
    z = dft_inverse(inv, p, uc, 0, uc, nb, sk, 0, F32)
    p = dft_forward(fwd, z, 0, spec, nb)
    y = dft_inverse(inv, p, z, 0, uc, 2 * nb, sk, nb, BF16)
    return y.reshape(batch * seq_len, HY_CH)


ML_CHUNK = 256
ML_GROWS = 8


def _split3_dot(x, tri):
    hi = x.astype(BF16)
    r1 = x - hi.astype(F32)
    mid = r1.astype(BF16)
    lo = (r1 - mid.astype(F32)).astype(BF16)
    return (jnp.dot(hi, tri, preferred_element_type=F32)
            + jnp.dot(mid, tri, preferred_element_type=F32)
            + jnp.dot(lo, tri, preferred_element_type=F32))


def _log_sigmoid(x):
    return jnp.minimum(x, 0.0) - jnp.log1p(jnp.exp(-jnp.abs(x)))


def _mlstm_kernel(q_ref, k_ref, v_ref, o_ref, g_ref, gb_ref, ng_ref, y_ref,
                  hf_ref, hb_ref, cf_ref, nf_ref, mf_ref, cb_ref, nb_ref, mb_ref, *, seq_len):
    lc = ML_CHUNK
    nc = seq_len // lc
    for ref in (cf_ref, nf_ref, mf_ref, cb_ref, nb_ref, mb_ref):
        ref[...] = jnp.zeros(ref.shape, ref.dtype)

    t_i = lax.broadcasted_iota(jnp.int32, (lc, lc), 0)
    s_i = lax.broadcasted_iota(jnp.int32, (lc, lc), 1)
    eye = t_i == s_i
    lower = s_i <= t_i
    upper = s_i >= t_i
    lower_b = lower.astype(BF16)
    upper_b = upper.astype(BF16)
    k_scale = ML_HEAD_DIM ** -0.5

    def to_col(row):
        return jnp.sum(jnp.where(eye, row, 0.0), axis=1, keepdims=True)

    def chunk(c, reverse, c_ref, n_ref, m_ref, h_ref):
        t0 = pl.multiple_of(c * lc, lc)
        q = q_ref[0, pl.ds(t0, lc), :]
        k = k_ref[0, pl.ds(t0, lc), :] * k_scale
        v = v_ref[0, pl.ds(t0, lc), :]
        g = g_ref[0, 0, :, pl.ds(t0, lc)] + gb_ref[0]
        r0 = 2 if reverse else 0
        li_row = g[r0:r0 + 1, :]
        lf_row = _log_sigmoid(g[r0 + 1:r0 + 2, :])
        b_row = _split3_dot(jnp.broadcast_to(lf_row, (ML_GROWS, lc)),
                            lower_b if reverse else upper_b)[0:1, :]
        seen = upper if reverse else lower
        b_col = to_col(b_row)
        li_col = to_col(li_row)
        m_old = m_ref[...]
        dm = jnp.where(seen, b_col - b_row + li_row, NEG)
        inter = b_col + m_old
        m_t = jnp.maximum(inter, jnp.max(dm, axis=1, keepdims=True))
        wm = jnp.exp(dm - m_t)
        a_inter = jnp.exp(inter - m_t)
        qk = lax.dot_general(q, k, (((1,), (1,)), ((), ())), preferred_element_type=F32)
        s = qk * wm
        c_old = c_ref[...]
        n_old = n_ref[...]
        num = (a_inter * jnp.dot(q, c_old.astype(BF16), preferred_element_type=F32)
               + jnp.dot(s.astype(BF16), v, preferred_element_type=F32))
        qn = jnp.sum(q.astype(F32) * n_old, axis=1, keepdims=True)
        den = a_inter * qn + jnp.sum(s, axis=1, keepdims=True)
        h_ref[pl.ds(t0, lc), :] = num / jnp.maximum(jnp.abs(den), jnp.exp(-m_t))
        b_last = b_row[:, 0:1] if reverse else b_row[:, lc - 1:lc]
        g_row = b_last - b_row + li_row
        g_col = b_last - b_col + li_col
        m_new = jnp.maximum(b_last + m_old, jnp.max(g_row, axis=1, keepdims=True))
        a_old = jnp.exp(b_last + m_old - m_new)
        kw = k.astype(F32) * jnp.exp(g_col - m_new)
        c_ref[...] = a_old * c_old + lax.dot_general(
            kw.astype(BF16), v, (((0,), (0,)), ((), ())), preferred_element_type=F32)
        n_ref[...] = a_old * n_old + jnp.sum(kw, axis=0, keepdims=True)
        m_ref[...] = m_new

    def step(c, carry):
        chunk(c, False, cf_ref, nf_ref, mf_ref, hf_ref)
        chunk(nc - 1 - c, True, cb_ref, nb_ref, mb_ref, hb_ref)
        return carry

    lax.fori_loop(0, nc, step, 0)

    def finish(c, carry):
        t0 = pl.multiple_of(c * lc, lc)
        hs = hf_ref[pl.ds(t0, lc), :] + hb_ref[pl.ds(t0, lc), :]
        ms = jnp.mean(hs * hs, axis=-1, keepdims=True)
        hs = hs * lax.rsqrt(ms + EPS) * ng_ref[...]
        gate = jax.nn.sigmoid(o_ref[0, pl.ds(t0, lc), :].astype(F32))
        y_ref[0, pl.ds(t0, lc), :] = (hs * gate).astype(y_ref.dtype)
        return carry

    lax.fori_loop(0, nc, finish, 0)


def mlstm_mixer(proj, gates, gate_b, norm_g, batch, seq_len):
    d = ML_HEAD_DIM
    pv = proj.reshape(batch, seq_len, proj.shape[-1])
    g = gates[:, :N_GATE].reshape(batch, seq_len, 4, ML_HEADS).transpose(0, 3, 2, 1)
    g = jnp.pad(g, ((0, 0), (0, 0), (0, ML_GROWS - 4), (0, 0)))
    gb = jnp.pad(gate_b.astype(F32).T, ((0, 0), (0, ML_GROWS - 4)))[..., None]
    blk = (1, seq_len, d)

    def col(off):
        return lambda b, h: (b, 0, off // d + h)

    y = pl.pallas_call(
        functools.partial(_mlstm_kernel, seq_len=seq_len),
        grid=(batch, ML_HEADS),
        in_specs=[pl.BlockSpec(blk, col(OFF_DQ)),
                  pl.BlockSpec(blk, col(OFF_DK)),
                  pl.BlockSpec(blk, col(OFF_DV)),
                  pl.BlockSpec(blk, col(OFF_DO)),
                  pl.BlockSpec((1, 1, ML_GROWS, seq_len), lambda b, h: (b, h, 0, 0)),
                  pl.BlockSpec((1, ML_GROWS, 1), lambda b, h: (h, 0, 0)),
                  pl.BlockSpec((1, d), lambda b, h: (0, h))],
        out_specs=pl.BlockSpec(blk, lambda b, h: (b, 0, h)),
        out_shape=jax.ShapeDtypeStruct((batch, seq_len, ML_HEADS * d), BF16),
        scratch_shapes=[pltpu.VMEM((seq_len, d), F32), pltpu.VMEM((seq_len, d), F32),
                        pltpu.VMEM((d, d), F32), pltpu.VMEM((1, d), F32), pltpu.VMEM((1, 1), F32),
                        pltpu.VMEM((d, d), F32), pltpu.VMEM((1, d), F32), pltpu.VMEM((1, 1), F32)],
        compiler_params=_cparams("parallel", "parallel"),
        name="mlstm",
    )(pv, pv, pv, pv, g, gb, norm_g.reshape(1, ML_HEADS * d).astype(F32))
    return y.reshape(batch * seq_len, ML_HEADS * d)


def _trunk(x, wts, shared):
    batch, seq_len, d = x.shape
    m = batch * seq_len
    ca, sa, cb, sb, fwd, inv = shared
    ones = jnp.ones((HEAD_DIM,), F32)
    q_scale = HEAD_DIM ** -0.5
    x2 = x.reshape(m, d)
    for l in range(DEPTH):
        w = wts[l]
        h = rmsnorm(x2, w["norm1_g"], BF16)
        proj = linear(h, w["w_in_main"], BF16, 1024, 512, name="in_proj")
        gates = linear(h, w["w_in_gate"], F32, 1024, LANES, name="in_proj_gates")
        a_q = rope_prep(proj, OFF_AQ, DA_HEADS, ca, sa, ones, seq_len,
                        half=ROPE_DIMS // 2, norm=False, scale=q_scale)
        a_k = rope_prep(proj, OFF_AK, DA_HEADS, ca, sa, ones, seq_len,
                        half=ROPE_DIMS // 2, norm=False, scale=1.0)
        parts = [dilated_pattern(a_q, a_k, proj, batch, seq_len, dil) for _, dil in DA_PATTERNS]
        y_a = merge_patterns([p[0] for p in parts], [p[1] for p in parts])
        b_q = rope_prep(proj, OFF_BQ, GQA_HEADS, cb, sb, w["qk_norm_g"][0], seq_len,
                        half=HEAD_DIM // 4, norm=True, scale=q_scale)
        b_k = rope_prep(proj, OFF_BK, GQA_KV_HEADS, cb, sb, w["qk_norm_g"][1], seq_len,
                        half=HEAD_DIM // 4, norm=True, scale=1.0)
        y_b = gqa_attention(b_q, b_k, proj, batch, seq_len)
        filt = hyena_filters(seq_len, w["hy_w1"], w["hy_b1"], w["hy_freq1"], w["hy_w2"],
                             w["hy_b2"], w["hy_freq2"], w["hy_w3"], w["hy_decay"])
        spec = linear(fwd, filt.astype(BF16), F32, DFT_TM, 512, name="hyena_filter_dft")
        y_c = hyena_mixer(proj, batch, seq_len, fwd, inv, spec,
                          w["hy_conv_w"], w["hy_conv_b"], w["hy_skip"])
        y_d = mlstm_mixer(proj, gates, w["ml_gate_b"], w["ml_norm_g"], batch, seq_len)
        y = jnp.concatenate([y_a, y_b, y_c, y_d], axis=-1)
        x2 = linear(y, w["w_out"], F32, 1024, 512, residual=x2, name="out_proj")
        h = rmsnorm(x2, w["norm2_g"], BF16)
        act = ffn_up(h, w["w_gate"], w["w_up"], 1024, 256)
        x2 = linear(act, w["w_down"], F32, 512, 256, residual=x2, name="ffn_down")
    return rmsnorm(x2, shared_final_g(wts), F32).reshape(batch, seq_len, d)


def shared_final_g(wts):
    return wts[0]["final_g"]


def kernel(x_prompt, x_sample, norm1_g, w_in, qk_norm_g, hy_conv_w, hy_conv_b, hy_w1, hy_b1,
           hy_freq1, hy_w2, hy_b2, hy_freq2, hy_w3, hy_decay, hy_skip, ml_gate_b, ml_norm_g,
           w_out, norm2_g, w_gate, w_up, w_down, final_g):
    wts = []
    for l in range(DEPTH):
        gate_cols = jnp.pad(w_in[l][:, N_MAIN:], ((0, 0), (0, LANES - N_GATE)))
        wts.append(dict(
            norm1_g=norm1_g[l], w_in_main=w_in[l][:, :N_MAIN].astype(BF16),
            w_in_gate=gate_cols.astype(BF16), qk_norm_g=qk_norm_g[l],
            hy_conv_w=hy_conv_w[l], hy_conv_b=hy_conv_b[l], hy_w1=hy_w1[l], hy_b1=hy_b1[l],
            hy_freq1=hy_freq1[l], hy_w2=hy_w2[l], hy_b2=hy_b2[l], hy_freq2=hy_freq2[l],
            hy_w3=hy_w3[l], hy_decay=hy_decay[l], hy_skip=hy_skip[l], ml_gate_b=ml_gate_b[l],
            ml_norm_g=ml_norm_g[l], w_out=w_out[l].astype(BF16), norm2_g=norm2_g[l],
            w_gate=w_gate[l].astype(BF16), w_up=w_up[l].astype(BF16),
            w_down=w_down[l].astype(BF16), final_g=final_g))
    outs = []
    for x in (x_prompt, x_sample):
        seq_len = x.shape[1]
        shared = rotary_tables(seq_len) + dft_matrices(seq_len)
        outs.append(_trunk(x, wts, shared))
    return tuple(outs)
```

```python
import functools
import math

import jax
import jax.numpy as jnp
from jax import lax
from jax.experimental import pallas as pl
from jax.experimental.pallas import tpu as pltpu

F32 = jnp.float32
BF16 = jnp.bfloat16

D_MODEL = 4096
DEPTH = 2
GROUP_WIDTH = 1024
HEAD_DIM = 128
DA_HEADS = 8
DA_PATTERNS = ((128, 1), (512, 4), (2048, 16))
ROPE_THETA = 500000.0
ROPE_DIMS = 32
GQA_HEADS = 8
GQA_KV_HEADS = 2
GQA_GROUP = GQA_HEADS // GQA_KV_HEADS
AXIAL_THETA = 10000.0
GRID_W = 64
HY_CH = 1024
HY_ORDER = 2
HY_EMB = 33
HY_BANDS = 16
ML_HEADS = 4
ML_HEAD_DIM = 256
FF_DIM = 11008
EPS = 1e-6
N_MAIN = 11776
N_GATE = 4 * ML_HEADS

OFF_AQ, OFF_AK, OFF_AV = 0, 1024, 2048
OFF_BQ, OFF_BK, OFF_BV = 3072, 4096, 4352
OFF_CU = 4608
OFF_DQ, OFF_DK, OFF_DV, OFF_DO = 7680, 8704, 9728, 10752

LANES = 128
NEG = -1e30
VMEM_LIMIT = 56 * 1024 * 1024


def _cparams(*sem):
    return pltpu.CompilerParams(dimension_semantics=sem, vmem_limit_bytes=VMEM_LIMIT)


def _rmsnorm_kernel(x_ref, g_ref, o_ref):
    x = x_ref[...].astype(F32)
    ms = jnp.mean(x * x, axis=-1, keepdims=True)
    o_ref[...] = (x * lax.rsqrt(ms + EPS) * g_ref[...]).astype(o_ref.dtype)


def rmsnorm(x, g, out_dtype, tm=256):
    m, d = x.shape
    return pl.pallas_call(
        _rmsnorm_kernel,
        grid=(m // tm,),
        in_specs=[pl.BlockSpec((tm, d), lambda i: (i, 0)),
                  pl.BlockSpec((1, d), lambda i: (0, 0))],
        out_specs=pl.BlockSpec((tm, d), lambda i: (i, 0)),
        out_shape=jax.ShapeDtypeStruct((m, d), out_dtype),
        compiler_params=_cparams("parallel"),
        name="rmsnorm",
    )(x, g.reshape(1, d).astype(F32))


def _linear_kernel(x_ref, w_ref, o_ref):
    o_ref[...] = jnp.dot(x_ref[...], w_ref[...], preferred_element_type=F32).astype(o_ref.dtype)


def _linear_res_kernel(x_ref, w_ref, r_ref, o_ref):
    acc = jnp.dot(x_ref[...], w_ref[...], preferred_element_type=F32)
    o_ref[...] = (r_ref[...] + acc).astype(o_ref.dtype)


def linear(x, w, out_dtype, tm, tn, residual=None, name="linear"):
    m, k = x.shape
    n = w.shape[1]
    in_specs = [pl.BlockSpec((tm, k), lambda i, j: (i, 0)),
                pl.BlockSpec((k, tn), lambda i, j: (0, j))]
    args = [x, w]
    body = _linear_kernel
    if residual is not None:
        in_specs.append(pl.BlockSpec((tm, tn), lambda i, j: (i, j)))
        args.append(residual)
        body = _linear_res_kernel
    return pl.pallas_call(
        body,
        grid=(m // tm, n // tn),
        in_specs=in_specs,
        out_specs=pl.BlockSpec((tm, tn), lambda i, j: (i, j)),
        out_shape=jax.ShapeDtypeStruct((m, n), out_dtype),
        compiler_params=_cparams("parallel", "arbitrary"),
        name=name,
    )(*args)


def _ffn_up_kernel(h_ref, wg_ref, wu_ref, o_ref):
    h = h_ref[...]
    g = jnp.dot(h, wg_ref[...], preferred_element_type=F32)
    u = jnp.dot(h, wu_ref[...], preferred_element_type=F32)
    o_ref[...] = (g * jax.nn.sigmoid(g) * u).astype(o_ref.dtype)


def ffn_up(h, wg, wu, tm, tn):
    m, k = h.shape
    n = wg.shape[1]
    return pl.pallas_call(
        _ffn_up_kernel,
        grid=(m // tm, n // tn),
        in_specs=[pl.BlockSpec((tm, k), lambda i, j: (i, 0)),
                  pl.BlockSpec((k, tn), lambda i, j: (0, j)),
                  pl.BlockSpec((k, tn), lambda i, j: (0, j))],
        out_specs=pl.BlockSpec((tm, tn), lambda i, j: (i, j)),
        out_shape=jax.ShapeDtypeStruct((m, n), BF16),
        compiler_params=_cparams("parallel", "arbitrary"),
        name="ffn_up",
    )(h, wg, wu)


def _rope_kernel(x_ref, c_ref, s_ref, g_ref, o_ref, *, heads, half, norm, scale):
    c = c_ref[...]
    s = s_ref[...]
    lane = lax.broadcasted_iota(jnp.int32, c.shape, 1)
    first = (lane % (2 * half)) < half
    for h in range(heads):
        sl = slice(h * HEAD_DIM, (h + 1) * HEAD_DIM)
        x = x_ref[:, sl].astype(F32)
        if norm:
            ms = jnp.mean(x * x, axis=-1, keepdims=True)
            x = x * lax.rsqrt(ms + EPS) * g_ref[...]
        partner = jnp.where(first, pltpu.roll(x, HEAD_DIM - half, 1), pltpu.roll(x, half, 1))
        o_ref[:, sl] = ((x * c + partner * s) * scale).astype(o_ref.dtype)


def rope_prep(proj, col_off, heads, cos_t, sin_t, g, seq_len, *, half, norm, scale, tm=512):
    m = proj.shape[0]
    width = heads * HEAD_DIM
    nblk = seq_len // tm
    kern = functools.partial(_rope_kernel, heads=heads, half=half, norm=norm, scale=scale)
    return pl.pallas_call(
        kern,
        grid=(m // tm,),
        in_specs=[pl.BlockSpec((tm, width), lambda i: (i, col_off // width)),
                  pl.BlockSpec((tm, HEAD_DIM), lambda i: (i % nblk, 0)),
                  pl.BlockSpec((tm, HEAD_DIM), lambda i: (i % nblk, 0)),
                  pl.BlockSpec((1, HEAD_DIM), lambda i: (0, 0))],
        out_specs=pl.BlockSpec((tm, width), lambda i: (i, 0)),
        out_shape=jax.ShapeDtypeStruct((m, width), BF16),
        compiler_params=_cparams("parallel"),
        name="rope_prep",
    )(proj, cos_t, sin_t, g.reshape(1, HEAD_DIM).astype(F32))


def _lane_tables(cos, sin, reps_after):
    c = jnp.concatenate([cos, cos], axis=-1)
    s = jnp.concatenate([-sin, sin], axis=-1)
    if reps_after:
        c = jnp.concatenate([c, jnp.ones((c.shape[0], reps_after), F32)], axis=-1)
        s = jnp.concatenate([s, jnp.zeros((s.shape[0], reps_after), F32)], axis=-1)
    return c, s


def _rope_angles(pos, dims, theta):
    inv = jnp.float32(theta) ** (-jnp.arange(0, dims, 2, dtype=F32) / dims)
    ang = pos[:, None] * inv[None, :]
    return jnp.cos(ang), jnp.sin(ang)


def rotary_tables(seq_len):
    cos_t, sin_t = _rope_angles(jnp.arange(seq_len, dtype=F32), ROPE_DIMS, ROPE_THETA)
    ca, sa = _lane_tables(cos_t, sin_t, HEAD_DIM - ROPE_DIMS)
    rows = seq_len // GRID_W
    row_pos = jnp.repeat(jnp.arange(rows, dtype=F32), GRID_W)
    col_pos = jnp.tile(jnp.arange(GRID_W, dtype=F32), rows)
    cr, sr = _lane_tables(*_rope_angles(row_pos, HEAD_DIM // 2, AXIAL_THETA), 0)
    cc, sc = _lane_tables(*_rope_angles(col_pos, HEAD_DIM // 2, AXIAL_THETA), 0)
    cb = jnp.concatenate([cr, cc], axis=-1)
    sb = jnp.concatenate([sr, sc], axis=-1)
    return ca, sa, cb, sb


DA_QBLK = 128
DA_HALF = 64


DA_ROWS = 256
DA_UNROLL = 4


def _dilated_kernel(q_ref, k_ref, v_ref, c_ref, s_ref, y_ref,
                    qf, kf, vf, qc, kc, vc, oc, lc, o_run, l_run, *, seq_len, scale):
    n = seq_len
    half = ROPE_DIMS // 2
    lane = lax.broadcasted_iota(jnp.int32, (DA_ROWS, HEAD_DIM), 1)
    first = (lane % (2 * half)) < half

    def rope(c, carry):
        rows = pl.ds(pl.multiple_of(c * DA_ROWS, DA_ROWS), DA_ROWS)
        cos, sin = c_ref[rows, :], s_ref[rows, :]
        for src, dst, mul in ((q_ref, qf, scale), (k_ref, kf, 1.0)):
            x = src[0, rows, :].astype(F32)
            partner = jnp.where(first, pltpu.roll(x, HEAD_DIM - half, 1), pltpu.roll(x, half, 1))
            dst[rows, :] = (x * cos + partner * sin) * mul
        vf[rows, :] = v_ref[0, rows, :].astype(F32)
        return carry

    lax.fori_loop(0, n // DA_ROWS, rope, 0)

    for pi, (_, dil) in enumerate(DA_PATTERNS):
        la = n // dil
        win = min(2 * DA_QBLK, la)
        per_class = la // DA_QBLK
        sub = min(DA_ROWS, la)

        def natural(r, c0, dil=dil, sub=sub):
            return pl.ds(r + dil * c0, sub, stride=dil) if dil > 1 else pl.ds(c0, sub)

        for r in range(dil):
            for c0 in range(0, la, sub):
                dst = pl.ds(r * la + c0, sub)
                qc[dst, :] = qf[natural(r, c0), :].astype(BF16)
                kc[dst, :] = kf[natural(r, c0), :].astype(BF16)
                vc[dst, :] = vf[natural(r, c0), :].astype(BF16)

        o_dst, l_dst = (o_run, l_run) if pi == 0 else (oc, lc)

        def band(g, carry, la=la, win=win, per_class=per_class, o_dst=o_dst, l_dst=l_dst):
            base = (g // per_class) * la
            a0 = (g % per_class) * DA_QBLK
            ws = jnp.clip(a0 - DA_HALF, 0, la - win)
            q_rows = pl.ds(pl.multiple_of(base + a0, DA_QBLK), DA_QBLK)
            k_rows = pl.ds(pl.multiple_of(base + ws, DA_HALF), win)
            s = lax.dot_general(qc[q_rows, :], kc[k_rows, :], (((1,), (1,)), ((), ())),
                                preferred_element_type=F32)
            qpos = a0 + lax.broadcasted_iota(jnp.int32, s.shape, 0)
            kpos = ws + lax.broadcasted_iota(jnp.int32, s.shape, 1)
            s = jnp.where(jnp.abs(kpos - qpos) <= DA_HALF, s, NEG)
            m = jnp.max(s, axis=-1, keepdims=True)
            p = jnp.exp(s - m)
            l = jnp.sum(p, axis=-1, keepdims=True)
            o = jnp.dot(p.astype(BF16), vc[k_rows, :], preferred_element_type=F32)
            o_dst[q_rows, :] = o / l
            l_dst[q_rows, :] = jnp.broadcast_to(m + jnp.log(l), (DA_QBLK, HEAD_DIM))
            return carry

        lax.fori_loop(0, n // DA_QBLK, band, 0, unroll=DA_UNROLL)

        if pi == 0:
            continue
        for r in range(dil):
            for c0 in range(0, la, sub):
                nat = natural(r, c0)
                cls = pl.ds(r * la + c0, sub)
                l_old, l_new = l_run[nat, :], lc[cls, :]
                mx = jnp.maximum(l_old, l_new)
                w_old, w_new = jnp.exp(l_old - mx), jnp.exp(l_new - mx)
                tot = w_old + w_new
                o_run[nat, :] = (w_old * o_run[nat, :] + w_new * oc[cls, :]) / tot
                l_run[nat, :] = mx + jnp.log(tot)

    def emit(c, carry):
        rows = pl.ds(pl.multiple_of(c * DA_ROWS, DA_ROWS), DA_ROWS)
        y_ref[0, rows, :] = o_run[rows, :].astype(y_ref.dtype)
        return carry

    lax.fori_loop(0, n // DA_ROWS, emit, 0)


def dilated_attention(proj, cos_t, sin_t, batch, seq_len):
    pv = proj.reshape(batch, seq_len, proj.shape[-1])
    blk = (1, seq_len, HEAD_DIM)

    def col(off):
        return lambda b, h: (b, 0, off // HEAD_DIM + h)

    table = pl.BlockSpec((seq_len, HEAD_DIM), lambda b, h: (0, 0))
    f32_rows = pltpu.VMEM((seq_len, HEAD_DIM), F32)
    bf16_rows = pltpu.VMEM((seq_len, HEAD_DIM), BF16)
    y = pl.pallas_call(
        functools.partial(_dilated_kernel, seq_len=seq_len, scale=HEAD_DIM ** -0.5),
        grid=(batch, DA_HEADS),
        in_specs=[pl.BlockSpec(blk, col(OFF_AQ)), pl.BlockSpec(blk, col(OFF_AK)),
                  pl.BlockSpec(blk, col(OFF_AV)), table, table],
        out_specs=pl.BlockSpec(blk, lambda b, h: (b, 0, h)),
        out_shape=jax.ShapeDtypeStruct((batch, seq_len, GROUP_WIDTH), BF16),
        scratch_shapes=[f32_rows] * 3 + [bf16_rows] * 3 + [f32_rows] * 4,
        compiler_params=_cparams("parallel", "parallel"),
        name="dilated_attention",
    )(pv, pv, pv, cos_t, sin_t)
    return y.reshape(batch * seq_len, GROUP_WIDTH)


GQA_SCORE_ELEMS = 2 * 1024 * 1024
LOG2E = 1.4426950408889634


def _gqa_kernel(q_ref, k_ref, v_ref, o_ref):
    tq = q_ref.shape[1]
    q = jnp.concatenate([q_ref[0, :, g * HEAD_DIM:(g + 1) * HEAD_DIM] for g in range(GQA_GROUP)],
                        axis=0)
    s = lax.dot_general(q, k_ref[0], (((1,), (1,)), ((), ())), preferred_element_type=F32)
    m = jnp.max(s, axis=-1, keepdims=True)
    p = jnp.exp2(s - m)
    l = jnp.sum(p, axis=-1, keepdims=True)
    o = jnp.dot(p.astype(BF16), v_ref[0], preferred_element_type=F32) / l
    for g in range(GQA_GROUP):
        o_ref[0, :, g * HEAD_DIM:(g + 1) * HEAD_DIM] = o[g * tq:(g + 1) * tq].astype(o_ref.dtype)


def gqa_attention(q, k, proj, batch, seq_len):
    tq = min(seq_len, GQA_SCORE_ELEMS // (GQA_GROUP * seq_len))
    qv = q.reshape(batch, seq_len, GQA_HEADS * HEAD_DIM)
    kv = k.reshape(batch, seq_len, GQA_KV_HEADS * HEAD_DIM)
    pv = proj.reshape(batch, seq_len, proj.shape[-1])
    gw = GQA_GROUP * HEAD_DIM
    v_blk0 = OFF_BV // HEAD_DIM
    y = pl.pallas_call(
        _gqa_kernel,
        grid=(batch, GQA_KV_HEADS, seq_len // tq),
        in_specs=[pl.BlockSpec((1, tq, gw), lambda b, h, i: (b, i, h)),
                  pl.BlockSpec((1, seq_len, HEAD_DIM), lambda b, h, i: (b, 0, h)),
                  pl.BlockSpec((1, seq_len, HEAD_DIM), lambda b, h, i: (b, 0, v_blk0 + h))],
        out_specs=pl.BlockSpec((1, tq, gw), lambda b, h, i: (b, i, h)),
        out_shape=jax.ShapeDtypeStruct((batch, seq_len, GQA_HEADS * HEAD_DIM), BF16),
        compiler_params=_cparams("parallel", "parallel", "arbitrary"),
        name="gqa_attention",
    )(qv, kv, pv)
    return y.reshape(batch * seq_len, GQA_HEADS * HEAD_DIM)


DFT_TM = 512
DFT_TN = 512


def _short_conv_kernel(u_ref, w_ref, b_ref, oe_ref, oo_ref, uf_ref):
    n = u_ref.shape[1]
    hn = n // 2
    uf_ref[...] = u_ref[0].astype(F32)
    ue = uf_ref[pl.ds(0, hn, stride=2), :]
    uo = uf_ref[pl.ds(1, hn, stride=2), :]
    row = lax.broadcasted_iota(jnp.int32, ue.shape, 0)
    uo_prev = jnp.where(row == 0, 0.0, pltpu.roll(uo, 1, 0))
    ue_next = jnp.where(row == hn - 1, 0.0, pltpu.roll(ue, hn - 1, 0))
    w0, w1, w2, b = w_ref[0:1, :], w_ref[1:2, :], w_ref[2:3, :], b_ref[...]
    oe_ref[0] = b + uo_prev * w0 + ue * w1 + uo * w2
    oo_ref[0] = b + ue * w0 + uo * w1 + ue_next * w2


def short_conv(proj, conv_w, conv_b, batch, seq_len):
    pv = proj.reshape(batch, seq_len, proj.shape[-1])
    width = conv_w.shape[-1]
    blk0 = OFF_CU // LANES
    out = jax.ShapeDtypeStruct((batch, seq_len // 2, width), F32)
    out_spec = pl.BlockSpec((1, seq_len // 2, LANES), lambda b, j: (b, 0, j))
    return pl.pallas_call(
        _short_conv_kernel,
        grid=(batch, width // LANES),
        in_specs=[pl.BlockSpec((1, seq_len, LANES), lambda b, j: (b, 0, blk0 + j)),
                  pl.BlockSpec((3, LANES), lambda b, j: (0, j)),
                  pl.BlockSpec((1, LANES), lambda b, j: (0, j))],
        out_specs=[out_spec, out_spec],
        out_shape=[out, out],
        scratch_shapes=[pltpu.VMEM((seq_len, LANES), F32)],
        compiler_params=_cparams("parallel", "parallel"),
        name="hyena_short_conv",
    )(pv, conv_w.astype(F32), conv_b.reshape(1, width).astype(F32))


def _interleave_kernel(e_ref, o_ref, y_ref, buf_ref):
    hn = e_ref.shape[1]
    buf_ref[pl.ds(0, hn, stride=2), :] = e_ref[0]
    buf_ref[pl.ds(1, hn, stride=2), :] = o_ref[0]
    y_ref[0] = buf_ref[...].astype(y_ref.dtype)


def interleave(even, odd, out_dtype):
    batch, hn, width = even.shape
    in_spec = pl.BlockSpec((1, hn, LANES), lambda b, j: (b, 0, j))
    return pl.pallas_call(
        _interleave_kernel,
        grid=(batch, width // LANES),
        in_specs=[in_spec, in_spec],
        out_specs=pl.BlockSpec((1, 2 * hn, LANES), lambda b, j: (b, 0, j)),
        out_shape=jax.ShapeDtypeStruct((batch, 2 * hn, width), out_dtype),
        scratch_shapes=[pltpu.VMEM((2 * hn, LANES), F32)],
        compiler_params=_cparams("parallel", "parallel"),
        name="hyena_interleave",
    )(even, odd)


def dft_matrices(seq_len):
    n = seq_len
    half = DFT_TM // 2
    i = jnp.arange(n, dtype=jnp.int32)
    tile, w = i // DFT_TM, i % DFT_TM
    imag = (w // half) == 1
    odd = 2 * (tile * half + w % half) + 1
    unit = math.pi / (2 * n)
    tables = []
    for parity in range(2):
        s = jnp.arange(parity, n, 2, dtype=jnp.int32)
        ang = ((odd[:, None] * s[None, :]) % (4 * n)).astype(F32) * unit
        tables.append(jnp.where(imag[:, None], -jnp.sin(ang), jnp.cos(ang)).astype(BF16))
    for parity in range(2):
        t = jnp.arange(parity, n, 2, dtype=jnp.int32) + n // 2
        ang = ((t[:, None] * odd[None, :]) % (4 * n)).astype(F32) * unit
        tables.append((jnp.where(imag[None, :], -jnp.sin(ang), jnp.cos(ang)) / n).astype(BF16))
    return tuple(tables)


def hyena_filters(seq_len, parity, w1, b1, fr1, w2, b2, fr2, w3, decay):
    hp = lax.Precision.HIGHEST
    n = jnp.arange(parity, seq_len, 2, dtype=F32)
    t = n / (seq_len - 1)
    f = jnp.linspace(1e-4, HY_BANDS - 1, HY_BANDS, dtype=F32)
    ang = (2.0 * math.pi / seq_len) * n[:, None] * f[None, :]
    z = jnp.concatenate([t[:, None], jnp.cos(ang), -jnp.sin(ang)], axis=-1)
    h = jnp.sin(fr1 * (jnp.dot(z, w1, precision=hp) + b1))
    h = jnp.sin(fr2 * (jnp.dot(h, w2, precision=hp) + b2))
    h = jnp.dot(h, w3, precision=hp)
    r = jnp.abs(n - seq_len // 2) / (seq_len // 2)
    return h * jnp.exp(-r[:, None] * decay[None, :])


def _cmul(ar, ai, br, bi):
    return ar * br - ai * bi, ar * bi + ai * br


def _dft_fwd_kernel(*refs, with_filter):
    if with_filter:
        fe_ref, fo_ref, ze_ref, zo_ref, h_ref, o_ref, zbe_ref, zbo_ref = refs
    else:
        fe_ref, fo_ref, ze_ref, zo_ref, o_ref, zbe_ref, zbo_ref = refs
    tn = ze_ref.shape[2]
    half = DFT_TM // 2

    @pl.when(pl.program_id(2) == 0)
    def _():
        zbe_ref[...] = ze_ref[0].astype(BF16)
        zbo_ref[...] = zo_ref[0].astype(BF16)

    a = jnp.dot(fe_ref[...], zbe_ref[...], preferred_element_type=F32)
    b = jnp.dot(fo_ref[...], zbo_ref[...], preferred_element_type=F32)
    u1r, u1i = a[:half] + b[:half], a[half:] + b[half:]
    u2r, u2i = a[:half] - b[:half], b[half:] - a[half:]
    if not with_filter:
        for q, val in enumerate((u1r, u1i, u2r, u2i)):
            o_ref[0, q * half:(q + 1) * half, :] = val
        return
    p1r, p1i = _cmul(u1r, u1i, h_ref[0:half, :], h_ref[half:2 * half, :])
    p2r, p2i = _cmul(u2r, u2i, h_ref[2 * half:3 * half, :], h_ref[3 * half:, :])
    o_ref[0, :half, :tn] = (p1r + p2r).astype(o_ref.dtype)
    o_ref[0, half:, :tn] = (p1i - p2i).astype(o_ref.dtype)
    o_ref[0, :half, tn:] = (p1r - p2r).astype(o_ref.dtype)
    o_ref[0, half:, tn:] = (p1i + p2i).astype(o_ref.dtype)


def dft_forward(tables, z_pair, z_blk0=0, spec=None, spec_blk0=0):
    batch, hn, _ = z_pair[0].shape
    seq_len = 2 * hn
    n_ch = HY_CH if spec is not None else z_pair[0].shape[2]
    f_spec = pl.BlockSpec((DFT_TM, hn), lambda b, j, i: (i, 0))
    z_spec = pl.BlockSpec((1, hn, DFT_TN), lambda b, j, i: (b, 0, z_blk0 + j))
    in_specs = [f_spec, f_spec, z_spec, z_spec]
    args = [tables[0], tables[1], z_pair[0], z_pair[1]]
    if spec is not None:
        in_specs.append(pl.BlockSpec((2 * DFT_TM, DFT_TN), lambda b, j, i: (i, spec_blk0 + j)))
        args.append(spec)
        out_spec = pl.BlockSpec((1, DFT_TM, 2 * DFT_TN), lambda b, j, i: (b, i, j))
        out_shape = jax.ShapeDtypeStruct((batch, seq_len, 2 * n_ch), BF16)
    else:
        out_spec = pl.BlockSpec((1, 2 * DFT_TM, DFT_TN), lambda b, j, i: (b, i, j))
        out_shape = jax.ShapeDtypeStruct((batch, 2 * seq_len, n_ch), F32)
    return pl.pallas_call(
        functools.partial(_dft_fwd_kernel, with_filter=spec is not None),
        grid=(batch, n_ch // DFT_TN, seq_len // DFT_TM),
        in_specs=in_specs,
        out_specs=out_spec,
        out_shape=out_shape,
        scratch_shapes=[pltpu.VMEM((hn, DFT_TN), BF16), pltpu.VMEM((hn, DFT_TN), BF16)],
        compiler_params=_cparams("parallel", "parallel", "arbitrary"),
        name="hyena_dft_fwd" if spec is not None else "hyena_filter_dft",
    )(*args)


def _dft_inv_kernel(ge_ref, go_ref, q_ref, ze_ref, zo_ref, xe_ref, xo_ref, sk_ref,
                    oe_ref, oo_ref):
    tn = ze_ref.shape[2]
    sk = sk_ref[...]
    ye = jnp.dot(ge_ref[...], q_ref[0, :, :tn], preferred_element_type=F32)
    oe_ref[0] = xe_ref[0] * (ye + sk * ze_ref[0])
    yo = jnp.dot(go_ref[...], q_ref[0, :, tn:], preferred_element_type=F32)
    oo_ref[0] = xo_ref[0] * (yo + sk * zo_ref[0])


def dft_inverse(tables, q, z_pair, z_blk0, gate_pair, gate_blk0, skip, skip_blk0, tm=256):
    batch, seq_len, _ = q.shape
    hn = seq_len // 2
    g_spec = pl.BlockSpec((tm, seq_len), lambda b, j, i: (i, 0))
    z_spec = pl.BlockSpec((1, tm, DFT_TN), lambda b, j, i: (b, i, z_blk0 + j))
    x_spec = pl.BlockSpec((1, tm, DFT_TN), lambda b, j, i: (b, i, gate_blk0 + j))
    out_spec = pl.BlockSpec((1, tm, DFT_TN), lambda b, j, i: (b, i, j))
    out = jax.ShapeDtypeStruct((batch, hn, HY_CH), F32)
    return pl.pallas_call(
        _dft_inv_kernel,
        grid=(batch, HY_CH // DFT_TN, hn // tm),
        in_specs=[g_spec, g_spec,
                  pl.BlockSpec((1, seq_len, 2 * DFT_TN), lambda b, j, i: (b, 0, j)),
                  z_spec, z_spec, x_spec, x_spec,
                  pl.BlockSpec((1, DFT_TN), lambda b, j, i: (0, skip_blk0 + j))],
        out_specs=[out_spec, out_spec],
        out_shape=[out, out],
        compiler_params=_cparams("parallel", "parallel", "arbitrary"),
        name="hyena_dft_inv",
    )(tables[2], tables[3], q, z_pair[0], z_pair[1], gate_pair[0], gate_pair[1], skip)


def hyena_mixer(proj, batch, seq_len, tables, spec, conv_w, conv_b, skip):
    uc = short_conv(proj, conv_w, conv_b, batch, seq_len)
    nb = HY_CH // DFT_TN
    sk = skip.reshape(1, HY_ORDER * HY_CH).astype(F32)
    q = dft_forward(tables, uc, 0, spec, 0)
    z = dft_inverse(tables, q, uc, 0, uc, nb, sk, 0)
    q = dft_forward(tables, z, 0, spec, nb)
    y = dft_inverse(tables, q, z, 0, uc, 2 * nb, sk, nb)
    return interleave(y[0], y[1], BF16).reshape(batch * seq_len, HY_CH)


ML_CHUNK = 256
ML_GROWS = 8


def _split3_dot(x, tri):
    hi = x.astype(BF16)
    r1 = x - hi.astype(F32)
    mid = r1.astype(BF16)
    lo = (r1 - mid.astype(F32)).astype(BF16)
    return (jnp.dot(hi, tri, preferred_element_type=F32)
            + jnp.dot(mid, tri, preferred_element_type=F32)
            + jnp.dot(lo, tri, preferred_element_type=F32))


def _log_sigmoid(x):
    return jnp.minimum(x, 0.0) - jnp.log1p(jnp.exp(-jnp.abs(x)))


def _mlstm_kernel(q_ref, k_ref, v_ref, o_ref, g_ref, gb_ref, ng_ref, y_ref,
                  hf_ref, hb_ref, cf_ref, nf_ref, mf_ref, cb_ref, nb_ref, mb_ref, *, seq_len):
    lc = ML_CHUNK
    nc = seq_len // lc
    for ref in (cf_ref, nf_ref, mf_ref, cb_ref, nb_ref, mb_ref):
        ref[...] = jnp.zeros(ref.shape, ref.dtype)

    t_i = lax.broadcasted_iota(jnp.int32, (lc, lc), 0)
    s_i = lax.broadcasted_iota(jnp.int32, (lc, lc), 1)
    eye = t_i == s_i
    lower = s_i <= t_i
    upper = s_i >= t_i
    lower_b = lower.astype(BF16)
    upper_b = upper.astype(BF16)
    k_scale = ML_HEAD_DIM ** -0.5

    def to_col(row):
        return jnp.sum(jnp.where(eye, row, 0.0), axis=1, keepdims=True)

    def chunk(c, reverse, c_ref, n_ref, m_ref, h_ref):
        t0 = pl.multiple_of(c * lc, lc)
        q = q_ref[0, pl.ds(t0, lc), :]
        k = k_ref[0, pl.ds(t0, lc), :] * k_scale
        v = v_ref[0, pl.ds(t0, lc), :]
        g = g_ref[0, 0, :, pl.ds(t0, lc)] + gb_ref[0]
        r0 = 2 if reverse else 0
        li_row = g[r0:r0 + 1, :]
        lf_row = _log_sigmoid(g[r0 + 1:r0 + 2, :])
        b_row = _split3_dot(jnp.broadcast_to(lf_row, (ML_GROWS, lc)),
                            lower_b if reverse else upper_b)[0:1, :]
        seen = upper if reverse else lower
        b_col = to_col(b_row)
        li_col = to_col(li_row)
        m_old = m_ref[...]
        dm = jnp.where(seen, b_col - b_row + li_row, NEG)
        inter = b_col + m_old
        m_t = jnp.maximum(inter, jnp.max(dm, axis=1, keepdims=True))
        wm = jnp.exp(dm - m_t)
        a_inter = jnp.exp(inter - m_t)
        qk = lax.dot_general(q, k, (((1,), (1,)), ((), ())), preferred_element_type=F32)
        s = qk * wm
        c_old = c_ref[...]
        n_old = n_ref[...]
        num = (a_inter * jnp.dot(q, c_old.astype(BF16), preferred_element_type=F32)
               + jnp.dot(s.astype(BF16), v, preferred_element_type=F32))
        qn = jnp.sum(q.astype(F32) * n_old, axis=1, keepdims=True)
        den = a_inter * qn + jnp.sum(s, axis=1, keepdims=True)
        h_ref[pl.ds(t0, lc), :] = num / jnp.maximum(jnp.abs(den), jnp.exp(-m_t))
        b_last = b_row[:, 0:1] if reverse else b_row[:, lc - 1:lc]
        g_row = b_last - b_row + li_row
        g_col = b_last - b_col + li_col
        m_new = jnp.maximum(b_last + m_old, jnp.max(g_row, axis=1, keepdims=True))
        a_old = jnp.exp(b_last + m_old - m_new)
        kw = k.astype(F32) * jnp.exp(g_col - m_new)
        c_ref[...] = a_old * c_old + lax.dot_general(
            kw.astype(BF16), v, (((0,), (0,)), ((), ())), preferred_element_type=F32)
        n_ref[...] = a_old * n_old + jnp.sum(kw, axis=0, keepdims=True)
        m_ref[...] = m_new

    def step(c, carry):
        chunk(c, False, cf_ref, nf_ref, mf_ref, hf_ref)
        chunk(nc - 1 - c, True, cb_ref, nb_ref, mb_ref, hb_ref)
        return carry

    lax.fori_loop(0, nc, step, 0)

    def finish(c, carry):
        t0 = pl.multiple_of(c * lc, lc)
        hs = hf_ref[pl.ds(t0, lc), :] + hb_ref[pl.ds(t0, lc), :]
        ms = jnp.mean(hs * hs, axis=-1, keepdims=True)
        hs = hs * lax.rsqrt(ms + EPS) * ng_ref[...]
        gate = jax.nn.sigmoid(o_ref[0, pl.ds(t0, lc), :].astype(F32))
        y_ref[0, pl.ds(t0, lc), :] = (hs * gate).astype(y_ref.dtype)
        return carry

    lax.fori_loop(0, nc, finish, 0)


def mlstm_mixer(proj, gates, gate_b, norm_g, batch, seq_len):
    d = ML_HEAD_DIM
    pv = proj.reshape(batch, seq_len, proj.shape[-1])
    g = gates[:, :N_GATE].reshape(batch, seq_len, 4, ML_HEADS).transpose(0, 3, 2, 1)
    g = jnp.pad(g, ((0, 0), (0, 0), (0, ML_GROWS - 4), (0, 0)))
    gb = jnp.pad(gate_b.astype(F32).T, ((0, 0), (0, ML_GROWS - 4)))[..., None]
    blk = (1, seq_len, d)

    def col(off):
        return lambda b, h: (b, 0, off // d + h)

    y = pl.pallas_call(
        functools.partial(_mlstm_kernel, seq_len=seq_len),
        grid=(batch, ML_HEADS),
        in_specs=[pl.BlockSpec(blk, col(OFF_DQ)),
                  pl.BlockSpec(blk, col(OFF_DK)),
                  pl.BlockSpec(blk, col(OFF_DV)),
                  pl.BlockSpec(blk, col(OFF_DO)),
                  pl.BlockSpec((1, 1, ML_GROWS, seq_len), lambda b, h: (b, h, 0, 0)),
                  pl.BlockSpec((1, ML_GROWS, 1), lambda b, h: (h, 0, 0)),
                  pl.BlockSpec((1, d), lambda b, h: (0, h))],
        out_specs=pl.BlockSpec(blk, lambda b, h: (b, 0, h)),
        out_shape=jax.ShapeDtypeStruct((batch, seq_len, ML_HEADS * d), BF16),
        scratch_shapes=[pltpu.VMEM((seq_len, d), F32), pltpu.VMEM((seq_len, d), F32),
                        pltpu.VMEM((d, d), F32), pltpu.VMEM((1, d), F32), pltpu.VMEM((1, 1), F32),
                        pltpu.VMEM((d, d), F32), pltpu.VMEM((1, d), F32), pltpu.VMEM((1, 1), F32)],
        compiler_params=_cparams("parallel", "parallel"),
        name="mlstm",
    )(pv, pv, pv, pv, g, gb, norm_g.reshape(1, ML_HEADS * d).astype(F32))
    return y.reshape(batch * seq_len, ML_HEADS * d)


def _trunk(x, wts, shared):
    batch, seq_len, d = x.shape
    m = batch * seq_len
    ca, sa, cb, sb = shared[:4]
    dft = shared[4:]
    q_scale = HEAD_DIM ** -0.5
    x2 = x.reshape(m, d)
    for l in range(DEPTH):
        w = wts[l]
        h = rmsnorm(x2, w["norm1_g"], BF16)
        proj = linear(h, w["w_in_main"], BF16, 1024, 512, name="in_proj")
        gates = linear(h, w["w_in_gate"], F32, 1024, LANES, name="in_proj_gates")
        y_a = dilated_attention(proj, ca, sa, batch, seq_len)
        b_q = rope_prep(proj, OFF_BQ, GQA_HEADS, cb, sb, w["qk_norm_g"][0], seq_len,
                        half=HEAD_DIM // 4, norm=True, scale=q_scale * LOG2E)
        b_k = rope_prep(proj, OFF_BK, GQA_KV_HEADS, cb, sb, w["qk_norm_g"][1], seq_len,
                        half=HEAD_DIM // 4, norm=True, scale=1.0)
        y_b = gqa_attention(b_q, b_k, proj, batch, seq_len)
        filt = [hyena_filters(seq_len, parity, w["hy_w1"], w["hy_b1"], w["hy_freq1"], w["hy_w2"],
                              w["hy_b2"], w["hy_freq2"], w["hy_w3"], w["hy_decay"])[None]
                for parity in range(2)]
        spec = dft_forward(dft, filt)[0]
        y_c = hyena_mixer(proj, batch, seq_len, dft, spec,
                          w["hy_conv_w"], w["hy_conv_b"], w["hy_skip"])
        y_d = mlstm_mixer(proj, gates, w["ml_gate_b"], w["ml_norm_g"], batch, seq_len)
        y = jnp.concatenate([y_a, y_b, y_c, y_d], axis=-1)
        x2 = linear(y, w["w_out"], F32, 1024, 512, residual=x2, name="out_proj")
        h = rmsnorm(x2, w["norm2_g"], BF16)
        act = ffn_up(h, w["w_gate"], w["w_up"], 1024, 256)
        x2 = linear(act, w["w_down"], F32, 512, 256, residual=x2, name="ffn_down")
    return rmsnorm(x2, shared_final_g(wts), F32).reshape(batch, seq_len, d)


def shared_final_g(wts):
    return wts[0]["final_g"]


def kernel(x_prompt, x_sample, norm1_g, w_in, qk_norm_g, hy_conv_w, hy_conv_b, hy_w1, hy_b1,
           hy_freq1, hy_w2, hy_b2, hy_freq2, hy_w3, hy_decay, hy_skip, ml_gate_b, ml_norm_g,
           w_out, norm2_g, w_gate, w_up, w_down, final_g):
    wts = []
    for l in range(DEPTH):
        gate_cols = jnp.pad(w_in[l][:, N_MAIN:], ((0, 0), (0, LANES - N_GATE)))
        wts.append(dict(
            norm1_g=norm1_g[l], w_in_main=w_in[l][:, :N_MAIN].astype(BF16),
            w_in_gate=gate_cols.astype(BF16), qk_norm_g=qk_norm_g[l],
            hy_conv_w=hy_conv_w[l], hy_conv_b=hy_conv_b[l], hy_w1=hy_w1[l], hy_b1=hy_b1[l],
            hy_freq1=hy_freq1[l], hy_w2=hy_w2[l], hy_b2=hy_b2[l], hy_freq2=hy_freq2[l],
            hy_w3=hy_w3[l], hy_decay=hy_decay[l], hy_skip=hy_skip[l], ml_gate_b=ml_gate_b[l],
            ml_norm_g=ml_norm_g[l], w_out=w_out[l].astype(BF16), norm2_g=norm2_g[l],
            w_gate=w_gate[l].astype(BF16), w_up=w_up[l].astype(BF16),
            w_down=w_down[l].astype(BF16), final_g=final_g))
    outs = []
    for x in (x_prompt, x_sample):
        seq_len = x.shape[1]
        shared = rotary_tables(seq_len) + dft_matrices(seq_len)
        outs.append(_trunk(x, wts, shared))
    return tuple(outs)
```

```python
import functools
import math

import jax
import jax.numpy as jnp
from jax import lax
from jax.experimental import pallas as pl
from jax.experimental.pallas import tpu as pltpu

F32 = jnp.float32
BF16 = jnp.bfloat16

D_MODEL = 4096
DEPTH = 2
GROUP_WIDTH = 1024
HEAD_DIM = 128
DA_HEADS = 8
DA_PATTERNS = ((128, 1), (512, 4), (2048, 16))
ROPE_THETA = 500000.0
ROPE_DIMS = 32
GQA_HEADS = 8
GQA_KV_HEADS = 2
GQA_GROUP = GQA_HEADS // GQA_KV_HEADS
AXIAL_THETA = 10000.0
GRID_W = 64
HY_CH = 1024
HY_ORDER = 2
HY_EMB = 33
HY_BANDS = 16
ML_HEADS = 4
ML_HEAD_DIM = 256
FF_DIM = 11008
EPS = 1e-6
N_MAIN = 11776
N_GATE = 4 * ML_HEADS

OFF_AQ, OFF_AK, OFF_AV = 0, 1024, 2048
OFF_BQ, OFF_BK, OFF_BV = 3072, 4096, 4352
OFF_CU = 4608
OFF_DQ, OFF_DK, OFF_DV, OFF_DO = 7680, 8704, 9728, 10752

LANES = 128
NEG = -1e30
VMEM_LIMIT = 56 * 1024 * 1024


def _cparams(*sem):
    return pltpu.CompilerParams(dimension_semantics=sem, vmem_limit_bytes=VMEM_LIMIT)


def _rmsnorm_kernel(x_ref, g_ref, o_ref):
    x = x_ref[...].astype(F32)
    ms = jnp.mean(x * x, axis=-1, keepdims=True)
    o_ref[...] = (x * lax.rsqrt(ms + EPS) * g_ref[...]).astype(o_ref.dtype)


def rmsnorm(x, g, out_dtype, tm=256):
    m, d = x.shape
    return pl.pallas_call(
        _rmsnorm_kernel,
        grid=(m // tm,),
        in_specs=[pl.BlockSpec((tm, d), lambda i: (i, 0)),
                  pl.BlockSpec((1, d), lambda i: (0, 0))],
        out_specs=pl.BlockSpec((tm, d), lambda i: (i, 0)),
        out_shape=jax.ShapeDtypeStruct((m, d), out_dtype),
        compiler_params=_cparams("parallel"),
        name="rmsnorm",
    )(x, g.reshape(1, d).astype(F32))


def _linear_kernel(x_ref, w_ref, o_ref):
    o_ref[...] = jnp.dot(x_ref[...], w_ref[...], preferred_element_type=F32).astype(o_ref.dtype)


def _linear_res_kernel(x_ref, w_ref, r_ref, o_ref):
    acc = jnp.dot(x_ref[...], w_ref[...], preferred_element_type=F32)
    o_ref[...] = (r_ref[...] + acc).astype(o_ref.dtype)


def linear(x, w, out_dtype, tm, tn, residual=None, name="linear"):
    m, k = x.shape
    n = w.shape[1]
    in_specs = [pl.BlockSpec((tm, k), lambda i, j: (i, 0)),
                pl.BlockSpec((k, tn), lambda i, j: (0, j))]
    args = [x, w]
    body = _linear_kernel
    if residual is not None:
        in_specs.append(pl.BlockSpec((tm, tn), lambda i, j: (i, j)))
        args.append(residual)
        body = _linear_res_kernel
    return pl.pallas_call(
        body,
        grid=(m // tm, n // tn),
        in_specs=in_specs,
        out_specs=pl.BlockSpec((tm, tn), lambda i, j: (i, j)),
        out_shape=jax.ShapeDtypeStruct((m, n), out_dtype),
        compiler_params=_cparams("parallel", "arbitrary"),
        name=name,
    )(*args)


def _ffn_up_kernel(h_ref, wg_ref, wu_ref, o_ref):
    h = h_ref[...]
    g = jnp.dot(h, wg_ref[...], preferred_element_type=F32)
    u = jnp.dot(h, wu_ref[...], preferred_element_type=F32)
    o_ref[...] = (g * jax.nn.sigmoid(g) * u).astype(o_ref.dtype)


def ffn_up(h, wg, wu, tm, tn):
    m, k = h.shape
    n = wg.shape[1]
    return pl.pallas_call(
        _ffn_up_kernel,
        grid=(m // tm, n // tn),
        in_specs=[pl.BlockSpec((tm, k), lambda i, j: (i, 0)),
                  pl.BlockSpec((k, tn), lambda i, j: (0, j)),
                  pl.BlockSpec((k, tn), lambda i, j: (0, j))],
        out_specs=pl.BlockSpec((tm, tn), lambda i, j: (i, j)),
        out_shape=jax.ShapeDtypeStruct((m, n), BF16),
        compiler_params=_cparams("parallel", "arbitrary"),
        name="ffn_up",
    )(h, wg, wu)


def _rope_kernel(x_ref, c_ref, s_ref, g_ref, o_ref, *, heads, half, norm, scale):
    c = c_ref[...]
    s = s_ref[...]
    lane = lax.broadcasted_iota(jnp.int32, c.shape, 1)
    first = (lane % (2 * half)) < half
    for h in range(heads):
        sl = slice(h * HEAD_DIM, (h + 1) * HEAD_DIM)
        x = x_ref[:, sl].astype(F32)
        if norm:
            ms = jnp.mean(x * x, axis=-1, keepdims=True)
            x = x * lax.rsqrt(ms + EPS) * g_ref[...]
        partner = jnp.where(first, pltpu.roll(x, HEAD_DIM - half, 1), pltpu.roll(x, half, 1))
        o_ref[:, sl] = ((x * c + partner * s) * scale).astype(o_ref.dtype)


def rope_prep(proj, col_off, heads, cos_t, sin_t, g, seq_len, *, half, norm, scale, tm=512):
    m = proj.shape[0]
    width = heads * HEAD_DIM
    nblk = seq_len // tm
    kern = functools.partial(_rope_kernel, heads=heads, half=half, norm=norm, scale=scale)
    return pl.pallas_call(
        kern,
        grid=(m // tm,),
        in_specs=[pl.BlockSpec((tm, width), lambda i: (i, col_off // width)),
                  pl.BlockSpec((tm, HEAD_DIM), lambda i: (i % nblk, 0)),
                  pl.BlockSpec((tm, HEAD_DIM), lambda i: (i % nblk, 0)),
                  pl.BlockSpec((1, HEAD_DIM), lambda i: (0, 0))],
        out_specs=pl.BlockSpec((tm, width), lambda i: (i, 0)),
        out_shape=jax.ShapeDtypeStruct((m, width), BF16),
        compiler_params=_cparams("parallel"),
        name="rope_prep",
    )(proj, cos_t, sin_t, g.reshape(1, HEAD_DIM).astype(F32))


def _lane_tables(cos, sin, reps_after):
    c = jnp.concatenate([cos, cos], axis=-1)
    s = jnp.concatenate([-sin, sin], axis=-1)
    if reps_after:
        c = jnp.concatenate([c, jnp.ones((c.shape[0], reps_after), F32)], axis=-1)
        s = jnp.concatenate([s, jnp.zeros((s.shape[0], reps_after), F32)], axis=-1)
    return c, s


def _rope_angles(pos, dims, theta):
    inv = jnp.float32(theta) ** (-jnp.arange(0, dims, 2, dtype=F32) / dims)
    ang = pos[:, None] * inv[None, :]
    return jnp.cos(ang), jnp.sin(ang)


def rotary_tables(seq_len):
    cos_t, sin_t = _rope_angles(jnp.arange(seq_len, dtype=F32), ROPE_DIMS, ROPE_THETA)
    ca, sa = _lane_tables(cos_t, sin_t, HEAD_DIM - ROPE_DIMS)
    rows = seq_len // GRID_W
    row_pos = jnp.repeat(jnp.arange(rows, dtype=F32), GRID_W)
    col_pos = jnp.tile(jnp.arange(GRID_W, dtype=F32), rows)
    cr, sr = _lane_tables(*_rope_angles(row_pos, HEAD_DIM // 2, AXIAL_THETA), 0)
    cc, sc = _lane_tables(*_rope_angles(col_pos, HEAD_DIM // 2, AXIAL_THETA), 0)
    cb = jnp.concatenate([cr, cc], axis=-1)
    sb = jnp.concatenate([sr, sc], axis=-1)
    return ca, sa, cb, sb


DA_QBLK = 128
DA_HALF = 64


DA_ROWS = 256
DA_UNROLL = 16


def _dilated_kernel(q_ref, k_ref, v_ref, c_ref, s_ref, y_ref,
                    qf, kf, vf, qg, kg, vg, qc, kc, vc, oc, lc, o_run, l_run, bias,
                    *, seq_len, scale):
    n = seq_len
    half = ROPE_DIMS // 2
    lane = lax.broadcasted_iota(jnp.int32, (DA_ROWS, HEAD_DIM), 1)
    first = (lane % (2 * half)) < half

    q_i = lax.broadcasted_iota(jnp.int32, (DA_QBLK, 2 * DA_QBLK), 0)
    k_j = lax.broadcasted_iota(jnp.int32, (DA_QBLK, 2 * DA_QBLK), 1)
    for c in range(3):
        bias[c] = jnp.where(jnp.abs(k_j - q_i - c * DA_HALF) <= DA_HALF, 0.0, NEG)

    def rope(c, carry):
        rows = pl.ds(pl.multiple_of(c * DA_ROWS, DA_ROWS), DA_ROWS)
        cos, sin = c_ref[rows, :], s_ref[rows, :]
        for src, dst, mul in ((q_ref, qf, scale), (k_ref, kf, 1.0)):
            x = src[0, rows, :].astype(F32)
            partner = jnp.where(first, pltpu.roll(x, HEAD_DIM - half, 1), pltpu.roll(x, half, 1))
            dst[rows, :] = (x * cos + partner * sin) * mul
        vf[rows, :] = v_ref[0, rows, :].astype(F32)
        return carry

    lax.fori_loop(0, n // DA_ROWS, rope, 0)

    f32_sets = ((qf, kf, vf), (qg, kg, vg))
    assert len(DA_PATTERNS) <= len(f32_sets) + 1
    for pi, (_, dil) in enumerate(DA_PATTERNS):
        la = n // dil
        win = min(2 * DA_QBLK, la)
        per_class = la // DA_QBLK
        sub = min(DA_ROWS, la)

        def natural(r, c0, dil=dil, sub=sub):
            return pl.ds(r + dil * c0, sub, stride=dil) if dil > 1 else pl.ds(c0, sub)

        prev_dil = DA_PATTERNS[pi - 1][1] if pi else 1
        ratio = dil // prev_dil
        sources = f32_sets[max(pi - 1, 0)]
        keep = f32_sets[pi] if 0 < pi < len(DA_PATTERNS) - 1 else (None,) * 3
        for r in range(dil):
            for c0 in range(0, la, sub):
                start = (r % prev_dil) * (n // prev_dil) + r // prev_dil + ratio * c0
                src = pl.ds(start, sub, stride=ratio) if ratio > 1 else pl.ds(start, sub)
                dst = pl.ds(r * la + c0, sub)
                for x_src, x_keep, x_dst in zip(sources, keep, (qc, kc, vc)):
                    x = x_src[src, :]
                    if x_keep is not None:
                        x_keep[dst, :] = x
                    x_dst[dst, :] = x.astype(BF16)

        o_dst, l_dst = (o_run, l_run) if pi == 0 else (oc, lc)

        def band(g, carry, la=la, win=win, per_class=per_class, o_dst=o_dst, l_dst=l_dst):
            base = (g // per_class) * la
            a0 = (g % per_class) * DA_QBLK
            ws = jnp.clip(a0 - DA_HALF, 0, la - win)
            q_rows = pl.ds(pl.multiple_of(base + a0, DA_QBLK), DA_QBLK)
            k_rows = pl.ds(pl.multiple_of(base + ws, DA_HALF), win)
            s = lax.dot_general(qc[q_rows, :], kc[k_rows, :], (((1,), (1,)), ((), ())),
                                preferred_element_type=F32)
            s = s + bias[(a0 - ws) // DA_HALF, :, 0:win]
            m = jnp.max(s, axis=-1, keepdims=True)
            p = jnp.exp(s - m)
            l = jnp.sum(p, axis=-1, keepdims=True)
            o = jnp.dot(p.astype(BF16), vc[k_rows, :], preferred_element_type=F32)
            o_dst[q_rows, :] = o / l
            l_dst[q_rows, :] = jnp.broadcast_to(m + jnp.log(l), (DA_QBLK, HEAD_DIM))
            return carry

        lax.fori_loop(0, n // DA_QBLK, band, 0, unroll=DA_UNROLL)

        if pi == 0:
            continue
        for r in range(dil):
            for c0 in range(0, la, sub):
                nat = natural(r, c0)
                cls = pl.ds(r * la + c0, sub)
                l_old, l_new = l_run[nat, :], lc[cls, :]
                mx = jnp.maximum(l_old, l_new)
                w_old, w_new = jnp.exp(l_old - mx), jnp.exp(l_new - mx)
                tot = w_old + w_new
                o_run[nat, :] = (w_old * o_run[nat, :] + w_new * oc[cls, :]) / tot
                l_run[nat, :] = mx + jnp.log(tot)

    def emit(c, carry):
        rows = pl.ds(pl.multiple_of(c * DA_ROWS, DA_ROWS), DA_ROWS)
        y_ref[0, rows, :] = o_run[rows, :].astype(y_ref.dtype)
        return carry

    lax.fori_loop(0, n // DA_ROWS, emit, 0)


def dilated_attention(proj, cos_t, sin_t, batch, seq_len):
    pv = proj.reshape(batch, seq_len, proj.shape[-1])
    blk = (1, seq_len, HEAD_DIM)

    def col(off):
        return lambda b, h: (b, 0, off // HEAD_DIM + h)

    table = pl.BlockSpec((seq_len, HEAD_DIM), lambda b, h: (0, 0))
    f32_rows = pltpu.VMEM((seq_len, HEAD_DIM), F32)
    bf16_rows = pltpu.VMEM((seq_len, HEAD_DIM), BF16)
    y = pl.pallas_call(
        functools.partial(_dilated_kernel, seq_len=seq_len, scale=HEAD_DIM ** -0.5),
        grid=(batch, DA_HEADS),
        in_specs=[pl.BlockSpec(blk, col(OFF_AQ)), pl.BlockSpec(blk, col(OFF_AK)),
                  pl.BlockSpec(blk, col(OFF_AV)), table, table],
        out_specs=pl.BlockSpec(blk, lambda b, h: (b, 0, h)),
        out_shape=jax.ShapeDtypeStruct((batch, seq_len, GROUP_WIDTH), BF16),
        scratch_shapes=[f32_rows] * 6 + [bf16_rows] * 3 + [f32_rows] * 4
        + [pltpu.VMEM((3, DA_QBLK, 2 * DA_QBLK), F32)],
        compiler_params=_cparams("parallel", "parallel"),
        name="dilated_attention",
    )(pv, pv, pv, cos_t, sin_t)
    return y.reshape(batch * seq_len, GROUP_WIDTH)


GQA_TQ = 256
LOG2E = 1.4426950408889634


def _gqa_kernel(q_ref, k_ref, v_ref, o_ref):
    k = k_ref[0]
    v = v_ref[0]
    for g in range(GQA_GROUP):
        sl = slice(g * HEAD_DIM, (g + 1) * HEAD_DIM)
        s = lax.dot_general(q_ref[0, :, sl], k, (((1,), (1,)), ((), ())),
                            preferred_element_type=F32)
        m = jnp.max(s, axis=-1, keepdims=True)
        p = jnp.exp2(s - m)
        l = jnp.sum(p, axis=-1, keepdims=True)
        o = jnp.dot(p.astype(BF16), v, preferred_element_type=F32)
        o_ref[0, :, sl] = (o / l).astype(o_ref.dtype)


def gqa_attention(q, k, proj, batch, seq_len):
    tq = min(seq_len, GQA_TQ)
    qv = q.reshape(batch, seq_len, GQA_HEADS * HEAD_DIM)
    kv = k.reshape(batch, seq_len, GQA_KV_HEADS * HEAD_DIM)
    pv = proj.reshape(batch, seq_len, proj.shape[-1])
    gw = GQA_GROUP * HEAD_DIM
    v_blk0 = OFF_BV // HEAD_DIM
    y = pl.pallas_call(
        _gqa_kernel,
        grid=(batch, GQA_KV_HEADS, seq_len // tq),
        in_specs=[pl.BlockSpec((1, tq, gw), lambda b, h, i: (b, i, h)),
                  pl.BlockSpec((1, seq_len, HEAD_DIM), lambda b, h, i: (b, 0, h)),
                  pl.BlockSpec((1, seq_len, HEAD_DIM), lambda b, h, i: (b, 0, v_blk0 + h))],
        out_specs=pl.BlockSpec((1, tq, gw), lambda b, h, i: (b, i, h)),
        out_shape=jax.ShapeDtypeStruct((batch, seq_len, GQA_HEADS * HEAD_DIM), BF16),
        compiler_params=_cparams("parallel", "parallel", "arbitrary"),
        name="gqa_attention",
    )(qv, kv, pv)
    return y.reshape(batch * seq_len, GQA_HEADS * HEAD_DIM)


DFT_TM = 512
DFT_TN = 512
DFT_COLS = 256


def _short_conv_kernel(u_ref, w_ref, b_ref, oe_ref, oo_ref, uf_ref):
    n = u_ref.shape[1]
    hn = n // 2
    uf_ref[...] = u_ref[0].astype(F32)
    ue = uf_ref[pl.ds(0, hn, stride=2), :]
    uo = uf_ref[pl.ds(1, hn, stride=2), :]
    row = lax.broadcasted_iota(jnp.int32, ue.shape, 0)
    uo_prev = jnp.where(row == 0, 0.0, pltpu.roll(uo, 1, 0))
    ue_next = jnp.where(row == hn - 1, 0.0, pltpu.roll(ue, hn - 1, 0))
    w0, w1, w2, b = w_ref[0:1, :], w_ref[1:2, :], w_ref[2:3, :], b_ref[...]
    oe_ref[0] = b + uo_prev * w0 + ue * w1 + uo * w2
    oo_ref[0] = b + ue * w0 + uo * w1 + ue_next * w2


def short_conv(proj, conv_w, conv_b, batch, seq_len):
    pv = proj.reshape(batch, seq_len, proj.shape[-1])
    width = conv_w.shape[-1]
    blk0 = OFF_CU // LANES
    out = jax.ShapeDtypeStruct((batch, seq_len // 2, width), F32)
    out_spec = pl.BlockSpec((1, seq_len // 2, LANES), lambda b, j: (b, 0, j))
    return pl.pallas_call(
        _short_conv_kernel,
        grid=(batch, width // LANES),
        in_specs=[pl.BlockSpec((1, seq_len, LANES), lambda b, j: (b, 0, blk0 + j)),
                  pl.BlockSpec((3, LANES), lambda b, j: (0, j)),
                  pl.BlockSpec((1, LANES), lambda b, j: (0, j))],
        out_specs=[out_spec, out_spec],
        out_shape=[out, out],
        scratch_shapes=[pltpu.VMEM((seq_len, LANES), F32)],
        compiler_params=_cparams("parallel", "parallel"),
        name="hyena_short_conv",
    )(pv, conv_w.astype(F32), conv_b.reshape(1, width).astype(F32))


def _interleave_kernel(e_ref, o_ref, y_ref, buf_ref):
    hn = e_ref.shape[1]
    buf_ref[pl.ds(0, hn, stride=2), :] = e_ref[0]
    buf_ref[pl.ds(1, hn, stride=2), :] = o_ref[0]
    y_ref[0] = buf_ref[...].astype(y_ref.dtype)


def interleave(even, odd, out_dtype):
    batch, hn, width = even.shape
    in_spec = pl.BlockSpec((1, hn, LANES), lambda b, j: (b, 0, j))
    return pl.pallas_call(
        _interleave_kernel,
        grid=(batch, width // LANES),
        in_specs=[in_spec, in_spec],
        out_specs=pl.BlockSpec((1, 2 * hn, LANES), lambda b, j: (b, 0, j)),
        out_shape=jax.ShapeDtypeStruct((batch, 2 * hn, width), out_dtype),
        scratch_shapes=[pltpu.VMEM((2 * hn, LANES), F32)],
        compiler_params=_cparams("parallel", "parallel"),
        name="hyena_interleave",
    )(even, odd)


def dft_matrices(seq_len):
    n = seq_len
    assert n % 4 == 0
    half = DFT_TM // 2
    i = jnp.arange(n, dtype=jnp.int32)
    tile, w = i // DFT_TM, i % DFT_TM
    imag = (w // half) == 1
    odd = 2 * (tile * half + w % half) + 1
    unit = math.pi / (2 * n)
    fwd = []
    for parity in range(2):
        s = jnp.arange(parity, n, 2, dtype=jnp.int32)
        ang = ((odd[:, None] * s[None, :]) % (4 * n)).astype(F32) * unit
        fwd.append(jnp.where(imag[:, None], -jnp.sin(ang), jnp.cos(ang)).astype(BF16))
    f = jnp.arange(n // 2, dtype=jnp.int32)
    fold = []
    for eighths in ((2 * f + 1) % 8, (2 * (n - 1 - f) + 1) % 8):
        ang = eighths.astype(F32) * (math.pi / 4)
        fold += [(jnp.cos(ang) / n)[:, None], (jnp.sin(ang) / n)[:, None]]
    return fwd[0], fwd[1], fwd[0].T, fwd[1].T, tuple(fold)


def hyena_filters(seq_len, parity, w1, b1, fr1, w2, b2, fr2, w3, decay):
    hp = lax.Precision.HIGHEST
    n = jnp.arange(parity, seq_len, 2, dtype=F32)
    t = n / (seq_len - 1)
    f = jnp.linspace(1e-4, HY_BANDS - 1, HY_BANDS, dtype=F32)
    ang = (2.0 * math.pi / seq_len) * n[:, None] * f[None, :]
    z = jnp.concatenate([t[:, None], jnp.cos(ang), -jnp.sin(ang)], axis=-1)
    h = jnp.sin(fr1 * (jnp.dot(z, w1, precision=hp) + b1))
    h = jnp.sin(fr2 * (jnp.dot(h, w2, precision=hp) + b2))
    h = jnp.dot(h, w3, precision=hp)
    r = jnp.abs(n - seq_len // 2) / (seq_len // 2)
    return h * jnp.exp(-r[:, None] * decay[None, :])


def _cmul(ar, ai, br, bi):
    return ar * br - ai * bi, ar * bi + ai * br


def _dft_fwd_kernel(*refs, with_filter):
    if with_filter:
        fe_ref, fo_ref, ze_ref, zo_ref, h_ref, o_ref, zbe_ref, zbo_ref = refs
    else:
        fe_ref, fo_ref, ze_ref, zo_ref, c1r, c1i, c2r, c2i, o_ref, zbe_ref, zbo_ref = refs
    tn = ze_ref.shape[2]
    half = DFT_TM // 2

    @pl.when(pl.program_id(2) == 0)
    def _():
        zbe_ref[...] = ze_ref[0].astype(BF16)
        zbo_ref[...] = zo_ref[0].astype(BF16)

    for c0 in range(0, tn, DFT_COLS):
        cs = slice(c0, c0 + DFT_COLS)
        qo = slice(tn + c0, tn + c0 + DFT_COLS)
        a = jnp.dot(fe_ref[...], zbe_ref[:, cs], preferred_element_type=F32)
        b = jnp.dot(fo_ref[...], zbo_ref[:, cs], preferred_element_type=F32)
        u1r, u1i = a[:half] + b[:half], a[half:] + b[half:]
        u2r, u2i = a[:half] - b[:half], b[half:] - a[half:]
        if not with_filter:
            u1r, u1i = _cmul(u1r, u1i, c1r[...], c1i[...])
            u2r, u2i = _cmul(u2r, u2i, c2r[...], c2i[...])
            for q, val in enumerate((u1r, u1i, u2r, u2i)):
                o_ref[0, q * half:(q + 1) * half, cs] = val
            continue
        p1r, p1i = _cmul(u1r, u1i, h_ref[0:half, cs], h_ref[half:2 * half, cs])
        p2r, p2i = _cmul(u2r, u2i, h_ref[2 * half:3 * half, cs], h_ref[3 * half:, cs])
        o_ref[0, :half, cs] = (p1r + p2r).astype(o_ref.dtype)
        o_ref[0, half:, cs] = (p1i - p2i).astype(o_ref.dtype)
        o_ref[0, :half, qo] = (p1r - p2r).astype(o_ref.dtype)
        o_ref[0, half:, qo] = (p1i + p2i).astype(o_ref.dtype)


def dft_forward(tables, z_pair, z_blk0=0, spec=None, spec_blk0=0):
    batch, hn, _ = z_pair[0].shape
    seq_len = 2 * hn
    n_ch = HY_CH if spec is not None else z_pair[0].shape[2]
    f_spec = pl.BlockSpec((DFT_TM, hn), lambda b, j, i: (i, 0))
    z_spec = pl.BlockSpec((1, hn, DFT_TN), lambda b, j, i: (b, 0, z_blk0 + j))
    in_specs = [f_spec, f_spec, z_spec, z_spec]
    args = [tables[0], tables[1], z_pair[0], z_pair[1]]
    if spec is not None:
        in_specs.append(pl.BlockSpec((2 * DFT_TM, DFT_TN), lambda b, j, i: (i, spec_blk0 + j)))
        args.append(spec)
        out_spec = pl.BlockSpec((1, DFT_TM, 2 * DFT_TN), lambda b, j, i: (b, i, j))
        out_shape = jax.ShapeDtypeStruct((batch, seq_len, 2 * n_ch), BF16)
    else:
        in_specs += [pl.BlockSpec((DFT_TM // 2, 1), lambda b, j, i: (i, 0))] * 4
        args += list(tables[4])
        out_spec = pl.BlockSpec((1, 2 * DFT_TM, DFT_TN), lambda b, j, i: (b, i, j))
        out_shape = jax.ShapeDtypeStruct((batch, 2 * seq_len, n_ch), F32)
    return pl.pallas_call(
        functools.partial(_dft_fwd_kernel, with_filter=spec is not None),
        grid=(batch, n_ch // DFT_TN, seq_len // DFT_TM),
        in_specs=in_specs,
        out_specs=out_spec,
        out_shape=out_shape,
        scratch_shapes=[pltpu.VMEM((hn, DFT_TN), BF16), pltpu.VMEM((hn, DFT_TN), BF16)],
        compiler_params=_cparams("parallel", "parallel", "arbitrary"),
        name="hyena_dft_fwd" if spec is not None else "hyena_filter_dft",
    )(*args)


def _dft_inv_kernel(ge_ref, go_ref, q_ref, ze_ref, zo_ref, xe_ref, xo_ref, sk_ref,
                    oe_ref, oo_ref):
    tn = ze_ref.shape[2]
    sk = sk_ref[...]
    ye = jnp.dot(ge_ref[...], q_ref[0, :, :tn], preferred_element_type=F32)
    oe_ref[0] = xe_ref[0] * (ye + sk * ze_ref[0])
    yo = jnp.dot(go_ref[...], q_ref[0, :, tn:], preferred_element_type=F32)
    oo_ref[0] = xo_ref[0] * (yo + sk * zo_ref[0])


def dft_inverse(tables, q, z_pair, z_blk0, gate_pair, gate_blk0, skip, skip_blk0, tm=256):
    batch, seq_len, _ = q.shape
    hn = seq_len // 2
    g_spec = pl.BlockSpec((tm, seq_len), lambda b, j, i: (i, 0))
    z_spec = pl.BlockSpec((1, tm, DFT_TN), lambda b, j, i: (b, i, z_blk0 + j))
    x_spec = pl.BlockSpec((1, tm, DFT_TN), lambda b, j, i: (b, i, gate_blk0 + j))
    out_spec = pl.BlockSpec((1, tm, DFT_TN), lambda b, j, i: (b, i, j))
    out = jax.ShapeDtypeStruct((batch, hn, HY_CH), F32)
    return pl.pallas_call(
        _dft_inv_kernel,
        grid=(batch, HY_CH // DFT_TN, hn // tm),
        in_specs=[g_spec, g_spec,
                  pl.BlockSpec((1, seq_len, 2 * DFT_TN), lambda b, j, i: (b, 0, j)),
                  z_spec, z_spec, x_spec, x_spec,
                  pl.BlockSpec((1, DFT_TN), lambda b, j, i: (0, skip_blk0 + j))],
        out_specs=[out_spec, out_spec],
        out_shape=[out, out],
        compiler_params=_cparams("parallel", "parallel", "arbitrary"),
        name="hyena_dft_inv",
    )(tables[2], tables[3], q, z_pair[0], z_pair[1], gate_pair[0], gate_pair[1], skip)


def hyena_mixer(proj, batch, seq_len, tables, spec, conv_w, conv_b, skip):
    uc = short_conv(proj, conv_w, conv_b, batch, seq_len)
    nb = HY_CH // DFT_TN
    sk = skip.reshape(1, HY_ORDER * HY_CH).astype(F32)
    q = dft_forward(tables, uc, 0, spec, 0)
    z = dft_inverse(tables, q, uc, 0, uc, nb, sk, 0)
    q = dft_forward(tables, z, 0, spec, nb)
    y = dft_inverse(tables, q, z, 0, uc, 2 * nb, sk, nb)
    return interleave(y[0], y[1], BF16).reshape(batch * seq_len, HY_CH)


ML_CHUNK = 256
ML_GROWS = 8


def _split3_dot(x, tri):
    hi = x.astype(BF16)
    r1 = x - hi.astype(F32)
    mid = r1.astype(BF16)
    lo = (r1 - mid.astype(F32)).astype(BF16)
    return (jnp.dot(hi, tri, preferred_element_type=F32)
            + jnp.dot(mid, tri, preferred_element_type=F32)
            + jnp.dot(lo, tri, preferred_element_type=F32))


def _log_sigmoid(x):
    return jnp.minimum(x, 0.0) - jnp.log1p(jnp.exp(-jnp.abs(x)))


def _mlstm_kernel(q_ref, k_ref, v_ref, o_ref, g_ref, gb_ref, ng_ref, y_ref,
                  hf_ref, hb_ref, cf_ref, nf_ref, mf_ref, cb_ref, nb_ref, mb_ref, *, seq_len):
    lc = ML_CHUNK
    nc = seq_len // lc
    for ref in (cf_ref, nf_ref, mf_ref, cb_ref, nb_ref, mb_ref):
        ref[...] = jnp.zeros(ref.shape, ref.dtype)

    t_i = lax.broadcasted_iota(jnp.int32, (lc, lc), 0)
    s_i = lax.broadcasted_iota(jnp.int32, (lc, lc), 1)
    eye = t_i == s_i
    lower = s_i <= t_i
    upper = s_i >= t_i
    lower_b = lower.astype(BF16)
    upper_b = upper.astype(BF16)
    k_scale = ML_HEAD_DIM ** -0.5

    def to_col(row):
        return jnp.sum(jnp.where(eye, row, 0.0), axis=1, keepdims=True)

    def chunk(c, reverse, c_ref, n_ref, m_ref, h_ref):
        t0 = pl.multiple_of(c * lc, lc)
        q = q_ref[0, pl.ds(t0, lc), :]
        k = k_ref[0, pl.ds(t0, lc), :] * k_scale
        v = v_ref[0, pl.ds(t0, lc), :]
        g = g_ref[0, 0, :, pl.ds(t0, lc)] + gb_ref[0]
        r0 = 2 if reverse else 0
        li_row = g[r0:r0 + 1, :]
        lf_row = _log_sigmoid(g[r0 + 1:r0 + 2, :])
        b_row = _split3_dot(jnp.broadcast_to(lf_row, (ML_GROWS, lc)),
                            lower_b if reverse else upper_b)[0:1, :]
        seen = upper if reverse else lower
        b_col = to_col(b_row)
        li_col = to_col(li_row)
        m_old = m_ref[...]
        dm = jnp.where(seen, b_col - b_row + li_row, NEG)
        inter = b_col + m_old
        m_t = jnp.maximum(inter, jnp.max(dm, axis=1, keepdims=True))
        wm = jnp.exp(dm - m_t)
        a_inter = jnp.exp(inter - m_t)
        qk = lax.dot_general(q, k, (((1,), (1,)), ((), ())), preferred_element_type=F32)
        s = qk * wm
        c_old = c_ref[...]
        n_old = n_ref[...]
        num = (a_inter * jnp.dot(q, c_old.astype(BF16), preferred_element_type=F32)
               + jnp.dot(s.astype(BF16), v, preferred_element_type=F32))
        qn = jnp.sum(q.astype(F32) * n_old, axis=1, keepdims=True)
        den = a_inter * qn + jnp.sum(s, axis=1, keepdims=True)
        h_ref[pl.ds(t0, lc), :] = num / jnp.maximum(jnp.abs(den), jnp.exp(-m_t))
        b_last = b_row[:, 0:1] if reverse else b_row[:, lc - 1:lc]
        g_row = b_last - b_row + li_row
        g_col = b_last - b_col + li_col
        m_new = jnp.maximum(b_last + m_old, jnp.max(g_row, axis=1, keepdims=True))
        a_old = jnp.exp(b_last + m_old - m_new)
        kw = k.astype(F32) * jnp.exp(g_col - m_new)
        c_ref[...] = a_old * c_old + lax.dot_general(
            kw.astype(BF16), v, (((0,), (0,)), ((), ())), preferred_element_type=F32)
        n_ref[...] = a_old * n_old + jnp.sum(kw, axis=0, keepdims=True)
        m_ref[...] = m_new

    def step(c, carry):
        chunk(c, False, cf_ref, nf_ref, mf_ref, hf_ref)
        chunk(nc - 1 - c, True, cb_ref, nb_ref, mb_ref, hb_ref)
        return carry

    lax.fori_loop(0, nc, step, 0)

    def finish(c, carry):
        t0 = pl.multiple_of(c * lc, lc)
        hs = hf_ref[pl.ds(t0, lc), :] + hb_ref[pl.ds(t0, lc), :]
        ms = jnp.mean(hs * hs, axis=-1, keepdims=True)
        hs = hs * lax.rsqrt(ms + EPS) * ng_ref[...]
        gate = jax.nn.sigmoid(o_ref[0, pl.ds(t0, lc), :].astype(F32))
        y_ref[0, pl.ds(t0, lc), :] = (hs * gate).astype(y_ref.dtype)
        return carry

    lax.fori_loop(0, nc, finish, 0)


def mlstm_mixer(proj, gates, gate_b, norm_g, batch, seq_len):
    d = ML_HEAD_DIM
    pv = proj.reshape(batch, seq_len, proj.shape[-1])
    g = gates[:, :N_GATE].reshape(batch, seq_len, 4, ML_HEADS).transpose(0, 3, 2, 1)
    g = jnp.pad(g, ((0, 0), (0, 0), (0, ML_GROWS - 4), (0, 0)))
    gb = jnp.pad(gate_b.astype(F32).T, ((0, 0), (0, ML_GROWS - 4)))[..., None]
    blk = (1, seq_len, d)

    def col(off):
        return lambda b, h: (b, 0, off // d + h)

    y = pl.pallas_call(
        functools.partial(_mlstm_kernel, seq_len=seq_len),
        grid=(batch, ML_HEADS),
        in_specs=[pl.BlockSpec(blk, col(OFF_DQ)),
                  pl.BlockSpec(blk, col(OFF_DK)),
                  pl.BlockSpec(blk, col(OFF_DV)),
                  pl.BlockSpec(blk, col(OFF_DO)),
                  pl.BlockSpec((1, 1, ML_GROWS, seq_len), lambda b, h: (b, h, 0, 0)),
                  pl.BlockSpec((1, ML_GROWS, 1), lambda b, h: (h, 0, 0)),
                  pl.BlockSpec((1, d), lambda b, h: (0, h))],
        out_specs=pl.BlockSpec(blk, lambda b, h: (b, 0, h)),
        out_shape=jax.ShapeDtypeStruct((batch, seq_len, ML_HEADS * d), BF16),
        scratch_shapes=[pltpu.VMEM((seq_len, d), F32), pltpu.VMEM((seq_len, d), F32),
                        pltpu.VMEM((d, d), F32), pltpu.VMEM((1, d), F32), pltpu.VMEM((1, 1), F32),
                        pltpu.VMEM((d, d), F32), pltpu.VMEM((1, d), F32), pltpu.VMEM((1, 1), F32)],
        compiler_params=_cparams("parallel", "parallel"),
        name="mlstm",
    )(pv, pv, pv, pv, g, gb, norm_g.reshape(1, ML_HEADS * d).astype(F32))
    return y.reshape(batch * seq_len, ML_HEADS * d)


def _trunk(x, wts, shared):
    batch, seq_len, d = x.shape
    m = batch * seq_len
    ca, sa, cb, sb = shared[:4]
    dft = shared[4:]
    q_scale = HEAD_DIM ** -0.5
    x2 = x.reshape(m, d)
    for l in range(DEPTH):
        w = wts[l]
        h = rmsnorm(x2, w["norm1_g"], BF16)
        proj = linear(h, w["w_in_main"], BF16, 1024, 512, name="in_proj")
        gates = linear(h, w["w_in_gate"], F32, 1024, LANES, name="in_proj_gates")
        y_a = dilated_attention(proj, ca, sa, batch, seq_len)
        b_q = rope_prep(proj, OFF_BQ, GQA_HEADS, cb, sb, w["qk_norm_g"][0], seq_len,
                        half=HEAD_DIM // 4, norm=True, scale=q_scale * LOG2E)
        b_k = rope_prep(proj, OFF_BK, GQA_KV_HEADS, cb, sb, w["qk_norm_g"][1], seq_len,
                        half=HEAD_DIM // 4, norm=True, scale=1.0)
        y_b = gqa_attention(b_q, b_k, proj, batch, seq_len)
        filt = [hyena_filters(seq_len, parity, w["hy_w1"], w["hy_b1"], w["hy_freq1"], w["hy_w2"],
                              w["hy_b2"], w["hy_freq2"], w["hy_w3"], w["hy_decay"])[None]
                for parity in range(2)]
        spec = dft_forward(dft, filt)[0]
        y_c = hyena_mixer(proj, batch, seq_len, dft, spec,
                          w["hy_conv_w"], w["hy_conv_b"], w["hy_skip"])
        y_d = mlstm_mixer(proj, gates, w["ml_gate_b"], w["ml_norm_g"], batch, seq_len)
        y = jnp.concatenate([y_a, y_b, y_c, y_d], axis=-1)
        x2 = linear(y, w["w_out"], F32, 1024, 512, residual=x2, name="out_proj")
        h = rmsnorm(x2, w["norm2_g"], BF16)
        act = ffn_up(h, w["w_gate"], w["w_up"], 1024, 256)
        x2 = linear(act, w["w_down"], F32, 512, 512, residual=x2, name="ffn_down")
    return rmsnorm(x2, shared_final_g(wts), F32).reshape(batch, seq_len, d)


def shared_final_g(wts):
    return wts[0]["final_g"]


def kernel(x_prompt, x_sample, norm1_g, w_in, qk_norm_g, hy_conv_w, hy_conv_b, hy_w1, hy_b1,
           hy_freq1, hy_w2, hy_b2, hy_freq2, hy_w3, hy_decay, hy_skip, ml_gate_b, ml_norm_g,
           w_out, norm2_g, w_gate, w_up, w_down, final_g):
    wts = []
    for l in range(DEPTH):
        gate_cols = jnp.pad(w_in[l][:, N_MAIN:], ((0, 0), (0, LANES - N_GATE)))
        wts.append(dict(
            norm1_g=norm1_g[l], w_in_main=w_in[l][:, :N_MAIN].astype(BF16),
            w_in_gate=gate_cols.astype(BF16), qk_norm_g=qk_norm_g[l],
            hy_conv_w=hy_conv_w[l], hy_conv_b=hy_conv_b[l], hy_w1=hy_w1[l], hy_b1=hy_b1[l],
            hy_freq1=hy_freq1[l], hy_w2=hy_w2[l], hy_b2=hy_b2[l], hy_freq2=hy_freq2[l],
            hy_w3=hy_w3[l], hy_decay=hy_decay[l], hy_skip=hy_skip[l], ml_gate_b=ml_gate_b[l],
            ml_norm_g=ml_norm_g[l], w_out=w_out[l].astype(BF16), norm2_g=norm2_g[l],
            w_gate=w_gate[l].astype(BF16), w_up=w_up[l].astype(BF16),
            w_down=w_down[l].astype(BF16), final_g=final_g))
    outs = []
    for x in (x_prompt, x_sample):
        seq_len = x.shape[1]
        shared = rotary_tables(seq_len) + dft_matrices(seq_len)
        outs.append(_trunk(x, wts, shared))
    return tuple(outs)
```

```python
import functools
import math

import jax
import jax.numpy as jnp
from jax import lax
from jax.experimental import pallas as pl
from jax.experimental.pallas import tpu as pltpu

F32 = jnp.float32
BF16 = jnp.bfloat16

D_MODEL = 4096
DEPTH = 2
GROUP_WIDTH = 1024
HEAD_DIM = 128
DA_HEADS = 8
DA_PATTERNS = ((128, 1), (512, 4), (2048, 16))
ROPE_THETA = 500000.0
ROPE_DIMS = 32
GQA_HEADS = 8
GQA_KV_HEADS = 2
GQA_GROUP = GQA_HEADS // GQA_KV_HEADS
AXIAL_THETA = 10000.0
GRID_W = 64
HY_CH = 1024
HY_ORDER = 2
HY_EMB = 33
HY_BANDS = 16
ML_HEADS = 4
ML_HEAD_DIM = 256
FF_DIM = 11008
EPS = 1e-6
N_MAIN = 11776
N_GATE = 4 * ML_HEADS

OFF_AQ, OFF_AK, OFF_AV = 0, 1024, 2048
OFF_BQ, OFF_BK, OFF_BV = 3072, 4096, 4352
OFF_CU = 4608
OFF_DQ, OFF_DK, OFF_DV, OFF_DO = 7680, 8704, 9728, 10752

LANES = 128
NEG = -1e30
VMEM_LIMIT = 56 * 1024 * 1024


def _cparams(*sem):
    return pltpu.CompilerParams(dimension_semantics=sem, vmem_limit_bytes=VMEM_LIMIT)


def _rmsnorm_kernel(x_ref, g_ref, o_ref):
    x = x_ref[...].astype(F32)
    ms = jnp.mean(x * x, axis=-1, keepdims=True)
    o_ref[...] = (x * lax.rsqrt(ms + EPS) * g_ref[...]).astype(o_ref.dtype)


def rmsnorm(x, g, out_dtype, tm=256):
    m, d = x.shape
    return pl.pallas_call(
        _rmsnorm_kernel,
        grid=(m // tm,),
        in_specs=[pl.BlockSpec((tm, d), lambda i: (i, 0)),
                  pl.BlockSpec((1, d), lambda i: (0, 0))],
        out_specs=pl.BlockSpec((tm, d), lambda i: (i, 0)),
        out_shape=jax.ShapeDtypeStruct((m, d), out_dtype),
        compiler_params=_cparams("parallel"),
        name="rmsnorm",
    )(x, g.reshape(1, d).astype(F32))


def _linear_kernel(x_ref, w_ref, o_ref):
    o_ref[...] = jnp.dot(x_ref[...], w_ref[...], preferred_element_type=F32).astype(o_ref.dtype)


def _linear_res_kernel(x_ref, w_ref, r_ref, o_ref):
    acc = jnp.dot(x_ref[...], w_ref[...], preferred_element_type=F32)
    o_ref[...] = (r_ref[...] + acc).astype(o_ref.dtype)


def linear(x, w, out_dtype, tm, tn, residual=None, name="linear"):
    m, k = x.shape
    n = w.shape[1]
    in_specs = [pl.BlockSpec((tm, k), lambda i, j: (i, 0)),
                pl.BlockSpec((k, tn), lambda i, j: (0, j))]
    args = [x, w]
    body = _linear_kernel
    if residual is not None:
        in_specs.append(pl.BlockSpec((tm, tn), lambda i, j: (i, j)))
        args.append(residual)
        body = _linear_res_kernel
    return pl.pallas_call(
        body,
        grid=(m // tm, n // tn),
        in_specs=in_specs,
        out_specs=pl.BlockSpec((tm, tn), lambda i, j: (i, j)),
        out_shape=jax.ShapeDtypeStruct((m, n), out_dtype),
        compiler_params=_cparams("parallel", "arbitrary"),
        name=name,
    )(*args)


def _ffn_up_kernel(h_ref, wg_ref, wu_ref, o_ref):
    h = h_ref[...]
    g = jnp.dot(h, wg_ref[...], preferred_element_type=F32)
    u = jnp.dot(h, wu_ref[...], preferred_element_type=F32)
    o_ref[...] = (g * jax.nn.sigmoid(g) * u).astype(o_ref.dtype)


def ffn_up(h, wg, wu, tm, tn):
    m, k = h.shape
    n = wg.shape[1]
    return pl.pallas_call(
        _ffn_up_kernel,
        grid=(m // tm, n // tn),
        in_specs=[pl.BlockSpec((tm, k), lambda i, j: (i, 0)),
                  pl.BlockSpec((k, tn), lambda i, j: (0, j)),
                  pl.BlockSpec((k, tn), lambda i, j: (0, j))],
        out_specs=pl.BlockSpec((tm, tn), lambda i, j: (i, j)),
        out_shape=jax.ShapeDtypeStruct((m, n), BF16),
        compiler_params=_cparams("parallel", "arbitrary"),
        name="ffn_up",
    )(h, wg, wu)


def _rope_kernel(x_ref, c_ref, s_ref, g_ref, o_ref, *, heads, half, norm, scale):
    c = c_ref[...]
    s = s_ref[...]
    lane = lax.broadcasted_iota(jnp.int32, c.shape, 1)
    first = (lane % (2 * half)) < half
    for h in range(heads):
        sl = slice(h * HEAD_DIM, (h + 1) * HEAD_DIM)
        x = x_ref[:, sl].astype(F32)
        if norm:
            ms = jnp.mean(x * x, axis=-1, keepdims=True)
            x = x * lax.rsqrt(ms + EPS) * g_ref[...]
        partner = jnp.where(first, pltpu.roll(x, HEAD_DIM - half, 1), pltpu.roll(x, half, 1))
        o_ref[:, sl] = ((x * c + partner * s) * scale).astype(o_ref.dtype)


def rope_prep(proj, col_off, heads, cos_t, sin_t, g, seq_len, *, half, norm, scale, tm=512):
    m = proj.shape[0]
    width = heads * HEAD_DIM
    nblk = seq_len // tm
    kern = functools.partial(_rope_kernel, heads=heads, half=half, norm=norm, scale=scale)
    return pl.pallas_call(
        kern,
        grid=(m // tm,),
        in_specs=[pl.BlockSpec((tm, width), lambda i: (i, col_off // width)),
                  pl.BlockSpec((tm, HEAD_DIM), lambda i: (i % nblk, 0)),
                  pl.BlockSpec((tm, HEAD_DIM), lambda i: (i % nblk, 0)),
                  pl.BlockSpec((1, HEAD_DIM), lambda i: (0, 0))],
        out_specs=pl.BlockSpec((tm, width), lambda i: (i, 0)),
        out_shape=jax.ShapeDtypeStruct((m, width), BF16),
        compiler_params=_cparams("parallel"),
        name="rope_prep",
    )(proj, cos_t, sin_t, g.reshape(1, HEAD_DIM).astype(F32))


def _lane_tables(cos, sin, reps_after):
    c = jnp.concatenate([cos, cos], axis=-1)
    s = jnp.concatenate([-sin, sin], axis=-1)
    if reps_after:
        c = jnp.concatenate([c, jnp.ones((c.shape[0], reps_after), F32)], axis=-1)
        s = jnp.concatenate([s, jnp.zeros((s.shape[0], reps_after), F32)], axis=-1)
    return c, s


def _rope_angles(pos, dims, theta):
    inv = jnp.float32(theta) ** (-jnp.arange(0, dims, 2, dtype=F32) / dims)
    ang = pos[:, None] * inv[None, :]
    return jnp.cos(ang), jnp.sin(ang)


def rotary_tables(seq_len):
    cos_t, sin_t = _rope_angles(jnp.arange(seq_len, dtype=F32), ROPE_DIMS, ROPE_THETA)
    ca, sa = _lane_tables(cos_t, sin_t, HEAD_DIM - ROPE_DIMS)
    rows = seq_len // GRID_W
    row_pos = jnp.repeat(jnp.arange(rows, dtype=F32), GRID_W)
    col_pos = jnp.tile(jnp.arange(GRID_W, dtype=F32), rows)
    cr, sr = _lane_tables(*_rope_angles(row_pos, HEAD_DIM // 2, AXIAL_THETA), 0)
    cc, sc = _lane_tables(*_rope_angles(col_pos, HEAD_DIM // 2, AXIAL_THETA), 0)
    cb = jnp.concatenate([cr, cc], axis=-1)
    sb = jnp.concatenate([sr, sc], axis=-1)
    return ca, sa, cb, sb


DA_QBLK = 128
DA_HALF = 64


DA_ROWS = 256
DA_UNROLL = 16


def _dilated_kernel(q_ref, k_ref, v_ref, c_ref, s_ref, y_ref,
                    qf, kf, vf, qg, kg, vg, qc, kc, vc, oc, lc, o_run, l_run, bias,
                    *, seq_len, scale):
    n = seq_len
    half = ROPE_DIMS // 2
    lane = lax.broadcasted_iota(jnp.int32, (DA_ROWS, HEAD_DIM), 1)
    first = (lane % (2 * half)) < half

    q_i = lax.broadcasted_iota(jnp.int32, (DA_QBLK, 2 * DA_QBLK), 0)
    k_j = lax.broadcasted_iota(jnp.int32, (DA_QBLK, 2 * DA_QBLK), 1)
    for c in range(3):
        bias[c] = jnp.where(jnp.abs(k_j - q_i - c * DA_HALF) <= DA_HALF, 0.0, NEG)

    def rope(c, carry):
        rows = pl.ds(pl.multiple_of(c * DA_ROWS, DA_ROWS), DA_ROWS)
        cos, sin = c_ref[rows, :], s_ref[rows, :]
        for src, dst, mul in ((q_ref, qf, scale), (k_ref, kf, 1.0)):
            x = src[0, rows, :].astype(F32)
            partner = jnp.where(first, pltpu.roll(x, HEAD_DIM - half, 1), pltpu.roll(x, half, 1))
            dst[rows, :] = (x * cos + partner * sin) * mul
        vf[rows, :] = v_ref[0, rows, :].astype(F32)
        return carry

    lax.fori_loop(0, n // DA_ROWS, rope, 0)

    f32_sets = ((qf, kf, vf), (qg, kg, vg))
    assert len(DA_PATTERNS) <= len(f32_sets) + 1
    for pi, (_, dil) in enumerate(DA_PATTERNS):
        la = n // dil
        win = min(2 * DA_QBLK, la)
        per_class = la // DA_QBLK
        sub = min(DA_ROWS, la)

        def natural(r, c0, dil=dil, sub=sub):
            return pl.ds(r + dil * c0, sub, stride=dil) if dil > 1 else pl.ds(c0, sub)

        prev_dil = DA_PATTERNS[pi - 1][1] if pi else 1
        ratio = dil // prev_dil
        sources = f32_sets[max(pi - 1, 0)]
        keep = f32_sets[pi] if 0 < pi < len(DA_PATTERNS) - 1 else (None,) * 3
        for r in range(dil):
            for c0 in range(0, la, sub):
                start = (r % prev_dil) * (n // prev_dil) + r // prev_dil + ratio * c0
                src = pl.ds(start, sub, stride=ratio) if ratio > 1 else pl.ds(start, sub)
                dst = pl.ds(r * la + c0, sub)
                for x_src, x_keep, x_dst in zip(sources, keep, (qc, kc, vc)):
                    x = x_src[src, :]
                    if x_keep is not None:
                        x_keep[dst, :] = x
                    x_dst[dst, :] = x.astype(BF16)

        o_dst, l_dst = (o_run, l_run) if pi == 0 else (oc, lc)

        def band(g, carry, la=la, win=win, per_class=per_class, o_dst=o_dst, l_dst=l_dst):
            base = (g // per_class) * la
            a0 = (g % per_class) * DA_QBLK
            ws = jnp.clip(a0 - DA_HALF, 0, la - win)
            q_rows = pl.ds(pl.multiple_of(base + a0, DA_QBLK), DA_QBLK)
            k_rows = pl.ds(pl.multiple_of(base + ws, DA_HALF), win)
            s = lax.dot_general(qc[q_rows, :], kc[k_rows, :], (((1,), (1,)), ((), ())),
                                preferred_element_type=F32)
            s = s + bias[(a0 - ws) // DA_HALF, :, 0:win]
            m = jnp.max(s, axis=-1, keepdims=True)
            p = jnp.exp(s - m)
            l = jnp.sum(p, axis=-1, keepdims=True)
            o = jnp.dot(p.astype(BF16), vc[k_rows, :], preferred_element_type=F32)
            o_dst[q_rows, :] = o / l
            l_dst[q_rows, :] = jnp.broadcast_to(m + jnp.log(l), (DA_QBLK, HEAD_DIM))
            return carry

        lax.fori_loop(0, n // DA_QBLK, band, 0, unroll=DA_UNROLL)

        if pi == 0:
            continue
        for r in range(dil):
            for c0 in range(0, la, sub):
                nat = natural(r, c0)
                cls = pl.ds(r * la + c0, sub)
                l_old, l_new = l_run[nat, :], lc[cls, :]
                mx = jnp.maximum(l_old, l_new)
                w_old, w_new = jnp.exp(l_old - mx), jnp.exp(l_new - mx)
                tot = w_old + w_new
                o_run[nat, :] = (w_old * o_run[nat, :] + w_new * oc[cls, :]) / tot
                l_run[nat, :] = mx + jnp.log(tot)

    def emit(c, carry):
        rows = pl.ds(pl.multiple_of(c * DA_ROWS, DA_ROWS), DA_ROWS)
        y_ref[0, rows, :] = o_run[rows, :].astype(y_ref.dtype)
        return carry

    lax.fori_loop(0, n // DA_ROWS, emit, 0)


def dilated_attention(proj, cos_t, sin_t, batch, seq_len):
    pv = proj.reshape(batch, seq_len, proj.shape[-1])
    blk = (1, seq_len, HEAD_DIM)

    def col(off):
        return lambda b, h: (b, 0, off // HEAD_DIM + h)

    table = pl.BlockSpec((seq_len, HEAD_DIM), lambda b, h: (0, 0))
    f32_rows = pltpu.VMEM((seq_len, HEAD_DIM), F32)
    bf16_rows = pltpu.VMEM((seq_len, HEAD_DIM), BF16)
    y = pl.pallas_call(
        functools.partial(_dilated_kernel, seq_len=seq_len, scale=HEAD_DIM ** -0.5),
        grid=(batch, DA_HEADS),
        in_specs=[pl.BlockSpec(blk, col(OFF_AQ)), pl.BlockSpec(blk, col(OFF_AK)),
                  pl.BlockSpec(blk, col(OFF_AV)), table, table],
        out_specs=pl.BlockSpec(blk, lambda b, h: (b, 0, h)),
        out_shape=jax.ShapeDtypeStruct((batch, seq_len, GROUP_WIDTH), BF16),
        scratch_shapes=[f32_rows] * 6 + [bf16_rows] * 3 + [f32_rows] * 4
        + [pltpu.VMEM((3, DA_QBLK, 2 * DA_QBLK), F32)],
        compiler_params=_cparams("parallel", "parallel"),
        name="dilated_attention",
    )(pv, pv, pv, cos_t, sin_t)
    return y.reshape(batch * seq_len, GROUP_WIDTH)


GQA_SCORE_ELEMS = 1024 * 1024
LOG2E = 1.4426950408889634


def _gqa_kernel(q_ref, k_ref, v_ref, o_ref):
    k = k_ref[0]
    v = v_ref[0]
    for g in range(GQA_GROUP):
        sl = slice(g * HEAD_DIM, (g + 1) * HEAD_DIM)
        s = lax.dot_general(q_ref[0, :, sl], k, (((1,), (1,)), ((), ())),
                            preferred_element_type=F32)
        m = jnp.max(s, axis=-1, keepdims=True)
        p = jnp.exp2(s - m)
        l = jnp.sum(p, axis=-1, keepdims=True)
        o = jnp.dot(p.astype(BF16), v, preferred_element_type=F32)
        o_ref[0, :, sl] = (o / l).astype(o_ref.dtype)


def gqa_attention(q, k, proj, batch, seq_len):
    tq = min(seq_len, GQA_SCORE_ELEMS // seq_len)
    qv = q.reshape(batch, seq_len, GQA_HEADS * HEAD_DIM)
    kv = k.reshape(batch, seq_len, GQA_KV_HEADS * HEAD_DIM)
    pv = proj.reshape(batch, seq_len, proj.shape[-1])
    gw = GQA_GROUP * HEAD_DIM
    v_blk0 = OFF_BV // HEAD_DIM
    y = pl.pallas_call(
        _gqa_kernel,
        grid=(batch, GQA_KV_HEADS, seq_len // tq),
        in_specs=[pl.BlockSpec((1, tq, gw), lambda b, h, i: (b, i, h)),
                  pl.BlockSpec((1, seq_len, HEAD_DIM), lambda b, h, i: (b, 0, h)),
                  pl.BlockSpec((1, seq_len, HEAD_DIM), lambda b, h, i: (b, 0, v_blk0 + h))],
        out_specs=pl.BlockSpec((1, tq, gw), lambda b, h, i: (b, i, h)),
        out_shape=jax.ShapeDtypeStruct((batch, seq_len, GQA_HEADS * HEAD_DIM), BF16),
        compiler_params=_cparams("parallel", "parallel", "arbitrary"),
        name="gqa_attention",
    )(qv, kv, pv)
    return y.reshape(batch * seq_len, GQA_HEADS * HEAD_DIM)


DFT_TM = 512
DFT_TN = 512
DFT_COLS = 256


HY_TC = 512


def _short_conv_kernel(u_ref, w_ref, b_ref, oe_ref, oo_ref, uf_ref):
    n = u_ref.shape[1]
    hn = n // 2
    row = lax.broadcasted_iota(jnp.int32, (hn, LANES), 0)
    for c in range(HY_TC // LANES):
        cs = slice(c * LANES, (c + 1) * LANES)
        uf_ref[c] = u_ref[0, :, cs].astype(F32)
        ue = uf_ref[c, pl.ds(0, hn, stride=2), :]
        uo = uf_ref[c, pl.ds(1, hn, stride=2), :]
        uo_prev = jnp.where(row == 0, 0.0, pltpu.roll(uo, 1, 0))
        ue_next = jnp.where(row == hn - 1, 0.0, pltpu.roll(ue, hn - 1, 0))
        w0, w1, w2, b = w_ref[0:1, cs], w_ref[1:2, cs], w_ref[2:3, cs], b_ref[:, cs]
        oe_ref[0, :, cs] = b + uo_prev * w0 + ue * w1 + uo * w2
        oo_ref[0, :, cs] = b + ue * w0 + uo * w1 + ue_next * w2


def short_conv(proj, conv_w, conv_b, batch, seq_len):
    pv = proj.reshape(batch, seq_len, proj.shape[-1])
    width = conv_w.shape[-1]
    blk0 = OFF_CU // HY_TC
    out = jax.ShapeDtypeStruct((batch, seq_len // 2, width), F32)
    out_spec = pl.BlockSpec((1, seq_len // 2, HY_TC), lambda b, j: (b, 0, j))
    return pl.pallas_call(
        _short_conv_kernel,
        grid=(batch, width // HY_TC),
        in_specs=[pl.BlockSpec((1, seq_len, HY_TC), lambda b, j: (b, 0, blk0 + j)),
                  pl.BlockSpec((3, HY_TC), lambda b, j: (0, j)),
                  pl.BlockSpec((1, HY_TC), lambda b, j: (0, j))],
        out_specs=[out_spec, out_spec],
        out_shape=[out, out],
        scratch_shapes=[pltpu.VMEM((HY_TC // LANES, seq_len, LANES), F32)],
        compiler_params=_cparams("parallel", "parallel"),
        name="hyena_short_conv",
    )(pv, conv_w.astype(F32), conv_b.reshape(1, width).astype(F32))


def _interleave_kernel(e_ref, o_ref, y_ref, buf_ref):
    hn = e_ref.shape[1]
    for c in range(HY_TC // LANES):
        cs = slice(c * LANES, (c + 1) * LANES)
        buf_ref[c, pl.ds(0, hn, stride=2), :] = e_ref[0, :, cs]
        buf_ref[c, pl.ds(1, hn, stride=2), :] = o_ref[0, :, cs]
        y_ref[0, :, cs] = buf_ref[c].astype(y_ref.dtype)


def interleave(even, odd, out_dtype):
    batch, hn, width = even.shape
    in_spec = pl.BlockSpec((1, hn, HY_TC), lambda b, j: (b, 0, j))
    return pl.pallas_call(
        _interleave_kernel,
        grid=(batch, width // HY_TC),
        in_specs=[in_spec, in_spec],
        out_specs=pl.BlockSpec((1, 2 * hn, HY_TC), lambda b, j: (b, 0, j)),
        out_shape=jax.ShapeDtypeStruct((batch, 2 * hn, width), out_dtype),
        scratch_shapes=[pltpu.VMEM((HY_TC // LANES, 2 * hn, LANES), F32)],
        compiler_params=_cparams("parallel", "parallel"),
        name="hyena_interleave",
    )(even, odd)


def dft_matrices(seq_len):
    n = seq_len
    assert n % 4 == 0
    half = DFT_TM // 2
    i = jnp.arange(n, dtype=jnp.int32)
    tile, w = i // DFT_TM, i % DFT_TM
    imag = (w // half) == 1
    odd = 2 * (tile * half + w % half) + 1
    unit = math.pi / (2 * n)
    fwd = []
    for parity in range(2):
        s = jnp.arange(parity, n, 2, dtype=jnp.int32)
        ang = ((odd[:, None] * s[None, :]) % (4 * n)).astype(F32) * unit
        fwd.append(jnp.where(imag[:, None], -jnp.sin(ang), jnp.cos(ang)).astype(BF16))
    fwd = lax.optimization_barrier(fwd)
    f = jnp.arange(n // 2, dtype=jnp.int32)
    fold = []
    for eighths in ((2 * f + 1) % 8, (2 * (n - 1 - f) + 1) % 8):
        ang = eighths.astype(F32) * (math.pi / 4)
        fold += [(jnp.cos(ang) / n)[:, None], (jnp.sin(ang) / n)[:, None]]
    return fwd[0], fwd[1], fwd[0].T, fwd[1].T, tuple(fold)


def hyena_filters(seq_len, parity, w1, b1, fr1, w2, b2, fr2, w3, decay):
    hp = lax.Precision.HIGHEST
    n = jnp.arange(parity, seq_len, 2, dtype=F32)
    t = n / (seq_len - 1)
    f = jnp.linspace(1e-4, HY_BANDS - 1, HY_BANDS, dtype=F32)
    ang = (2.0 * math.pi / seq_len) * n[:, None] * f[None, :]
    z = jnp.concatenate([t[:, None], jnp.cos(ang), -jnp.sin(ang)], axis=-1)
    h = jnp.sin(fr1 * (jnp.dot(z, w1, precision=hp) + b1))
    h = jnp.sin(fr2 * (jnp.dot(h, w2, precision=hp) + b2))
    h = jnp.dot(h, w3, precision=hp)
    r = jnp.abs(n - seq_len // 2) / (seq_len // 2)
    return h * jnp.exp(-r[:, None] * decay[None, :])


def _cmul(ar, ai, br, bi):
    return ar * br - ai * bi, ar * bi + ai * br


def _dft_fwd_kernel(*refs, with_filter):
    if with_filter:
        fe_ref, fo_ref, ze_ref, zo_ref, h_ref, o_ref, zbe_ref, zbo_ref = refs
    else:
        fe_ref, fo_ref, ze_ref, zo_ref, c1r, c1i, c2r, c2i, o_ref, zbe_ref, zbo_ref = refs
    tn = ze_ref.shape[2]
    half = DFT_TM // 2

    @pl.when(pl.program_id(2) == 0)
    def _():
        zbe_ref[...] = ze_ref[0].astype(BF16)
        zbo_ref[...] = zo_ref[0].astype(BF16)

    for c0 in range(0, tn, DFT_COLS):
        cs = slice(c0, c0 + DFT_COLS)
        qo = slice(tn + c0, tn + c0 + DFT_COLS)
        a = jnp.dot(fe_ref[...], zbe_ref[:, cs], preferred_element_type=F32)
        b = jnp.dot(fo_ref[...], zbo_ref[:, cs], preferred_element_type=F32)
        u1r, u1i = a[:half] + b[:half], a[half:] + b[half:]
        u2r, u2i = a[:half] - b[:half], b[half:] - a[half:]
        if not with_filter:
            u1r, u1i = _cmul(u1r, u1i, c1r[...], c1i[...])
            u2r, u2i = _cmul(u2r, u2i, c2r[...], c2i[...])
            for q, val in enumerate((u1r, u1i, u2r, u2i)):
                o_ref[0, q * half:(q + 1) * half, cs] = val
            continue
        p1r, p1i = _cmul(u1r, u1i, h_ref[0:half, cs], h_ref[half:2 * half, cs])
        p2r, p2i = _cmul(u2r, u2i, h_ref[2 * half:3 * half, cs], h_ref[3 * half:, cs])
        o_ref[0, :half, cs] = (p1r + p2r).astype(o_ref.dtype)
        o_ref[0, half:, cs] = (p1i - p2i).astype(o_ref.dtype)
        o_ref[0, :half, qo] = (p1r - p2r).astype(o_ref.dtype)
        o_ref[0, half:, qo] = (p1i + p2i).astype(o_ref.dtype)


def dft_forward(tables, z_pair, z_blk0=0, spec=None, spec_blk0=0):
    batch, hn, _ = z_pair[0].shape
    seq_len = 2 * hn
    n_ch = HY_CH if spec is not None else z_pair[0].shape[2]
    f_spec = pl.BlockSpec((DFT_TM, hn), lambda b, j, i: (i, 0))
    z_spec = pl.BlockSpec((1, hn, DFT_TN), lambda b, j, i: (b, 0, z_blk0 + j))
    in_specs = [f_spec, f_spec, z_spec, z_spec]
    args = [tables[0], tables[1], z_pair[0], z_pair[1]]
    if spec is not None:
        in_specs.append(pl.BlockSpec((2 * DFT_TM, DFT_TN), lambda b, j, i: (i, spec_blk0 + j)))
        args.append(spec)
        out_spec = pl.BlockSpec((1, DFT_TM, 2 * DFT_TN), lambda b, j, i: (b, i, j))
        out_shape = jax.ShapeDtypeStruct((batch, seq_len, 2 * n_ch), BF16)
    else:
        in_specs += [pl.BlockSpec((DFT_TM // 2, 1), lambda b, j, i: (i, 0))] * 4
        args += list(tables[4])
        out_spec = pl.BlockSpec((1, 2 * DFT_TM, DFT_TN), lambda b, j, i: (b, i, j))
        out_shape = jax.ShapeDtypeStruct((batch, 2 * seq_len, n_ch), F32)
    return pl.pallas_call(
        functools.partial(_dft_fwd_kernel, with_filter=spec is not None),
        grid=(batch, n_ch // DFT_TN, seq_len // DFT_TM),
        in_specs=in_specs,
        out_specs=out_spec,
        out_shape=out_shape,
        scratch_shapes=[pltpu.VMEM((hn, DFT_TN), BF16), pltpu.VMEM((hn, DFT_TN), BF16)],
        compiler_params=_cparams("parallel", "parallel", "arbitrary"),
        name="hyena_dft_fwd" if spec is not None else "hyena_filter_dft",
    )(*args)


def _dft_inv_kernel(ge_ref, go_ref, q_ref, ze_ref, zo_ref, xe_ref, xo_ref, sk_ref,
                    oe_ref, oo_ref):
    tn = ze_ref.shape[2]
    sk = sk_ref[...]
    ye = jnp.dot(ge_ref[...], q_ref[0, :, :tn], preferred_element_type=F32)
    oe_ref[0] = xe_ref[0] * (ye + sk * ze_ref[0])
    yo = jnp.dot(go_ref[...], q_ref[0, :, tn:], preferred_element_type=F32)
    oo_ref[0] = xo_ref[0] * (yo + sk * zo_ref[0])


def dft_inverse(tables, q, z_pair, z_blk0, gate_pair, gate_blk0, skip, skip_blk0, tm=256):
    batch, seq_len, _ = q.shape
    hn = seq_len // 2
    g_spec = pl.BlockSpec((tm, seq_len), lambda b, j, i: (i, 0))
    z_spec = pl.BlockSpec((1, tm, DFT_TN), lambda b, j, i: (b, i, z_blk0 + j))
    x_spec = pl.BlockSpec((1, tm, DFT_TN), lambda b, j, i: (b, i, gate_blk0 + j))
    out_spec = pl.BlockSpec((1, tm, DFT_TN), lambda b, j, i: (b, i, j))
    out = jax.ShapeDtypeStruct((batch, hn, HY_CH), F32)
    return pl.pallas_call(
        _dft_inv_kernel,
        grid=(batch, HY_CH // DFT_TN, hn // tm),
        in_specs=[g_spec, g_spec,
                  pl.BlockSpec((1, seq_len, 2 * DFT_TN), lambda b, j, i: (b, 0, j)),
                  z_spec, z_spec, x_spec, x_spec,
                  pl.BlockSpec((1, DFT_TN), lambda b, j, i: (0, skip_blk0 + j))],
        out_specs=[out_spec, out_spec],
        out_shape=[out, out],
        compiler_params=_cparams("parallel", "parallel", "arbitrary"),
        name="hyena_dft_inv",
    )(tables[2], tables[3], q, z_pair[0], z_pair[1], gate_pair[0], gate_pair[1], skip)


def hyena_mixer(proj, batch, seq_len, tables, spec, conv_w, conv_b, skip):
    uc = short_conv(proj, conv_w, conv_b, batch, seq_len)
    nb = HY_CH // DFT_TN
    sk = skip.reshape(1, HY_ORDER * HY_CH).astype(F32)
    q = dft_forward(tables, uc, 0, spec, 0)
    z = dft_inverse(tables, q, uc, 0, uc, nb, sk, 0)
    q = dft_forward(tables, z, 0, spec, nb)
    y = dft_inverse(tables, q, z, 0, uc, 2 * nb, sk, nb)
    return interleave(y[0], y[1], BF16).reshape(batch * seq_len, HY_CH)


ML_CHUNK = 256
ML_GROWS = 8


def _split3_dot(x, tri):
    hi = x.astype(BF16)
    r1 = x - hi.astype(F32)
    mid = r1.astype(BF16)
    lo = (r1 - mid.astype(F32)).astype(BF16)
    return (jnp.dot(hi, tri, preferred_element_type=F32)
            + jnp.dot(mid, tri, preferred_element_type=F32)
            + jnp.dot(lo, tri, preferred_element_type=F32))


def _log_sigmoid(x):
    return jnp.minimum(x, 0.0) - jnp.log1p(jnp.exp(-jnp.abs(x)))


def _mlstm_kernel(q_ref, k_ref, v_ref, o_ref, g_ref, gb_ref, ng_ref, y_ref,
                  hf_ref, hb_ref, cf_ref, nf_ref, mf_ref, cb_ref, nb_ref, mb_ref, *, seq_len):
    lc = ML_CHUNK
    nc = seq_len // lc
    for ref in (cf_ref, nf_ref, mf_ref, cb_ref, nb_ref, mb_ref):
        ref[...] = jnp.zeros(ref.shape, ref.dtype)

    t_i = lax.broadcasted_iota(jnp.int32, (lc, lc), 0)
    s_i = lax.broadcasted_iota(jnp.int32, (lc, lc), 1)
    eye = t_i == s_i
    lower = s_i <= t_i
    upper = s_i >= t_i
    lower_b = lower.astype(BF16)
    upper_b = upper.astype(BF16)
    k_scale = ML_HEAD_DIM ** -0.5

    def to_col(row):
        return jnp.sum(jnp.where(eye, row, 0.0), axis=1, keepdims=True)

    def chunk(c, reverse, c_ref, n_ref, m_ref, h_ref):
        t0 = pl.multiple_of(c * lc, lc)
        q = q_ref[0, pl.ds(t0, lc), :]
        k = k_ref[0, pl.ds(t0, lc), :] * k_scale
        v = v_ref[0, pl.ds(t0, lc), :]
        g = g_ref[:, pl.ds(t0, lc)] + gb_ref[0]
        r0 = 2 if reverse else 0
        li_row = g[r0:r0 + 1, :]
        lf_row = _log_sigmoid(g[r0 + 1:r0 + 2, :])
        b_row = _split3_dot(jnp.broadcast_to(lf_row, (ML_GROWS, lc)),
                            lower_b if reverse else upper_b)[0:1, :]
        seen = upper if reverse else lower
        b_col = to_col(b_row)
        li_col = to_col(li_row)
        m_old = m_ref[...]
        dm = jnp.where(seen, b_col - b_row + li_row, NEG)
        inter = b_col + m_old
        m_t = jnp.maximum(inter, jnp.max(dm, axis=1, keepdims=True))
        wm = jnp.exp(dm - m_t)
        a_inter = jnp.exp(inter - m_t)
        qk = lax.dot_general(q, k, (((1,), (1,)), ((), ())), preferred_element_type=F32)
        s = qk * wm
        c_old = c_ref[...]
        n_old = n_ref[...]
        num = (a_inter * jnp.dot(q, c_old.astype(BF16), preferred_element_type=F32)
               + jnp.dot(s.astype(BF16), v, preferred_element_type=F32))
        qn = jnp.sum(q.astype(F32) * n_old, axis=1, keepdims=True)
        den = a_inter * qn + jnp.sum(s, axis=1, keepdims=True)
        h_ref[pl.ds(t0, lc), :] = num / jnp.maximum(jnp.abs(den), jnp.exp(-m_t))
        b_last = b_row[:, 0:1] if reverse else b_row[:, lc - 1:lc]
        g_row = b_last - b_row + li_row
        g_col = b_last - b_col + li_col
        m_new = jnp.maximum(b_last + m_old, jnp.max(g_row, axis=1, keepdims=True))
        a_old = jnp.exp(b_last + m_old - m_new)
        kw = k.astype(F32) * jnp.exp(g_col - m_new)
        c_ref[...] = a_old * c_old + lax.dot_general(
            kw.astype(BF16), v, (((0,), (0,)), ((), ())), preferred_element_type=F32)
        n_ref[...] = a_old * n_old + jnp.sum(kw, axis=0, keepdims=True)
        m_ref[...] = m_new

    def step(c, carry):
        chunk(c, False, cf_ref, nf_ref, mf_ref, hf_ref)
        chunk(nc - 1 - c, True, cb_ref, nb_ref, mb_ref, hb_ref)
        return carry

    lax.fori_loop(0, nc, step, 0)

    def finish(c, carry):
        t0 = pl.multiple_of(c * lc, lc)
        hs = hf_ref[pl.ds(t0, lc), :] + hb_ref[pl.ds(t0, lc), :]
        ms = jnp.mean(hs * hs, axis=-1, keepdims=True)
        hs = hs * lax.rsqrt(ms + EPS) * ng_ref[...]
        gate = jax.nn.sigmoid(o_ref[0, pl.ds(t0, lc), :].astype(F32))
        y_ref[0, pl.ds(t0, lc), :] = (hs * gate).astype(y_ref.dtype)
        return carry

    lax.fori_loop(0, nc, finish, 0)


def _gate_proj_kernel(w_ref, h_ref, o_ref):
    o_ref[...] = lax.dot_general(w_ref[...], h_ref[...], (((1,), (1,)), ((), ())),
                                 preferred_element_type=F32)


def gate_rows_weight(w_gate_cols):
    d = w_gate_cols.shape[0]
    w = w_gate_cols.T.reshape(4, ML_HEADS, d).transpose(1, 0, 2)
    w = jnp.pad(w, ((0, 0), (0, ML_GROWS - 4), (0, 0)))
    return w.reshape(ML_HEADS * ML_GROWS, d).astype(BF16)


def gate_projection(h, w_rows, tm=1024):
    m, d = h.shape
    rows = w_rows.shape[0]
    return pl.pallas_call(
        _gate_proj_kernel,
        grid=(m // tm,),
        in_specs=[pl.BlockSpec((rows, d), lambda i: (0, 0)),
                  pl.BlockSpec((tm, d), lambda i: (i, 0))],
        out_specs=pl.BlockSpec((rows, tm), lambda i: (0, i)),
        out_shape=jax.ShapeDtypeStruct((rows, m), F32),
        compiler_params=_cparams("parallel"),
        name="in_proj_gates",
    )(w_rows, h)


def mlstm_mixer(proj, gates_t, gate_b, norm_g, batch, seq_len):
    d = ML_HEAD_DIM
    pv = proj.reshape(batch, seq_len, proj.shape[-1])
    gb = jnp.pad(gate_b.astype(F32).T, ((0, 0), (0, ML_GROWS - 4)))[..., None]
    blk = (1, seq_len, d)

    def col(off):
        return lambda b, h: (b, 0, off // d + h)

    y = pl.pallas_call(
        functools.partial(_mlstm_kernel, seq_len=seq_len),
        grid=(batch, ML_HEADS),
        in_specs=[pl.BlockSpec(blk, col(OFF_DQ)),
                  pl.BlockSpec(blk, col(OFF_DK)),
                  pl.BlockSpec(blk, col(OFF_DV)),
                  pl.BlockSpec(blk, col(OFF_DO)),
                  pl.BlockSpec((ML_GROWS, seq_len), lambda b, h: (h, b)),
                  pl.BlockSpec((1, ML_GROWS, 1), lambda b, h: (h, 0, 0)),
                  pl.BlockSpec((1, d), lambda b, h: (0, h))],
        out_specs=pl.BlockSpec(blk, lambda b, h: (b, 0, h)),
        out_shape=jax.ShapeDtypeStruct((batch, seq_len, ML_HEADS * d), BF16),
        scratch_shapes=[pltpu.VMEM((seq_len, d), F32), pltpu.VMEM((seq_len, d), F32),
                        pltpu.VMEM((d, d), F32), pltpu.VMEM((1, d), F32), pltpu.VMEM((1, 1), F32),
                        pltpu.VMEM((d, d), F32), pltpu.VMEM((1, d), F32), pltpu.VMEM((1, 1), F32)],
        compiler_params=_cparams("parallel", "parallel"),
        name="mlstm",
    )(pv, pv, pv, pv, gates_t, gb, norm_g.reshape(1, ML_HEADS * d).astype(F32))
    return y.reshape(batch * seq_len, ML_HEADS * d)


def _trunk(x, wts, shared):
    batch, seq_len, d = x.shape
    m = batch * seq_len
    ca, sa, cb, sb = shared[:4]
    dft = shared[4:]
    q_scale = HEAD_DIM ** -0.5
    x2 = x.reshape(m, d)
    for l in range(DEPTH):
        w = wts[l]
        h = rmsnorm(x2, w["norm1_g"], BF16)
        proj = linear(h, w["w_in_main"], BF16, 2048, 512, name="in_proj")
        gates = gate_projection(h, w["w_in_gate"])
        y_a = dilated_attention(proj, ca, sa, batch, seq_len)
        b_q = rope_prep(proj, OFF_BQ, GQA_HEADS, cb, sb, w["qk_norm_g"][0], seq_len,
                        half=HEAD_DIM // 4, norm=True, scale=q_scale * LOG2E)
        b_k = rope_prep(proj, OFF_BK, GQA_KV_HEADS, cb, sb, w["qk_norm_g"][1], seq_len,
                        half=HEAD_DIM // 4, norm=True, scale=1.0)
        y_b = gqa_attention(b_q, b_k, proj, batch, seq_len)
        filt = [hyena_filters(seq_len, parity, w["hy_w1"], w["hy_b1"], w["hy_freq1"], w["hy_w2"],
                              w["hy_b2"], w["hy_freq2"], w["hy_w3"], w["hy_decay"])[None]
                for parity in range(2)]
        spec = dft_forward(dft, filt)[0]
        y_c = hyena_mixer(proj, batch, seq_len, dft, spec,
                          w["hy_conv_w"], w["hy_conv_b"], w["hy_skip"])
        y_d = mlstm_mixer(proj, gates, w["ml_gate_b"], w["ml_norm_g"], batch, seq_len)
        y = jnp.concatenate([y_a, y_b, y_c, y_d], axis=-1)
        x2 = linear(y, w["w_out"], F32, 1024, 512, residual=x2, name="out_proj")
        h = rmsnorm(x2, w["norm2_g"], BF16)
        act = ffn_up(h, w["w_gate"], w["w_up"], 2048, 256)
        x2 = linear(act, w["w_down"], F32, 512, 512, residual=x2, name="ffn_down")
    return rmsnorm(x2, shared_final_g(wts), F32).reshape(batch, seq_len, d)


def shared_final_g(wts):
    return wts[0]["final_g"]


def kernel(x_prompt, x_sample, norm1_g, w_in, qk_norm_g, hy_conv_w, hy_conv_b, hy_w1, hy_b1,
           hy_freq1, hy_w2, hy_b2, hy_freq2, hy_w3, hy_decay, hy_skip, ml_gate_b, ml_norm_g,
           w_out, norm2_g, w_gate, w_up, w_down, final_g):
    wts = []
    for l in range(DEPTH):
        wts.append(dict(
            norm1_g=norm1_g[l], w_in_main=w_in[l][:, :N_MAIN].astype(BF16),
            w_in_gate=gate_rows_weight(w_in[l][:, N_MAIN:]), qk_norm_g=qk_norm_g[l],
            hy_conv_w=hy_conv_w[l], hy_conv_b=hy_conv_b[l], hy_w1=hy_w1[l], hy_b1=hy_b1[l],
            hy_freq1=hy_freq1[l], hy_w2=hy_w2[l], hy_b2=hy_b2[l], hy_freq2=hy_freq2[l],
            hy_w3=hy_w3[l], hy_decay=hy_decay[l], hy_skip=hy_skip[l], ml_gate_b=ml_gate_b[l],
            ml_norm_g=ml_norm_g[l], w_out=w_out[l].astype(BF16), norm2_g=norm2_g[l],
            w_gate=w_gate[l].astype(BF16), w_up=w_up[l].astype(BF16),
            w_down=w_down[l].astype(BF16), final_g=final_g))
    outs = []
    for x in (x_prompt, x_sample):
        seq_len = x.shape[1]
        shared = rotary_tables(seq_len) + dft_matrices(seq_len)
        outs.append(_trunk(x, wts, shared))
    return tuple(outs)
```

```python
import functools
import math

import jax
import jax.numpy as jnp
from jax import lax
from jax.experimental import pallas as pl
from jax.experimental.pallas import tpu as pltpu

F32 = jnp.float32
BF16 = jnp.bfloat16

D_MODEL = 4096
DEPTH = 2
GROUP_WIDTH = 1024
HEAD_DIM = 128
DA_HEADS = 8
DA_PATTERNS = ((128, 1), (512, 4), (2048, 16))
ROPE_THETA = 500000.0
ROPE_DIMS = 32
GQA_HEADS = 8
GQA_KV_HEADS = 2
GQA_GROUP = GQA_HEADS // GQA_KV_HEADS
AXIAL_THETA = 10000.0
GRID_W = 64
HY_CH = 1024
HY_ORDER = 2
HY_EMB = 33
HY_BANDS = 16
ML_HEADS = 4
ML_HEAD_DIM = 256
FF_DIM = 11008
EPS = 1e-6
N_MAIN = 11776
N_GATE = 4 * ML_HEADS

OFF_AQ, OFF_AK, OFF_AV = 0, 1024, 2048
OFF_BQ, OFF_BK, OFF_BV = 3072, 4096, 4352
OFF_CU = 4608
OFF_DQ, OFF_DK, OFF_DV, OFF_DO = 7680, 8704, 9728, 10752

LANES = 128
NEG = -1e30
VMEM_LIMIT = 56 * 1024 * 1024


def _cparams(*sem):
    return pltpu.CompilerParams(dimension_semantics=sem, vmem_limit_bytes=VMEM_LIMIT)


def _rmsnorm_kernel(x_ref, g_ref, o_ref):
    x = x_ref[...].astype(F32)
    ms = jnp.mean(x * x, axis=-1, keepdims=True)
    o_ref[...] = (x * lax.rsqrt(ms + EPS) * g_ref[...]).astype(o_ref.dtype)


def rmsnorm(x, g, out_dtype, tm=256):
    m, d = x.shape
    return pl.pallas_call(
        _rmsnorm_kernel,
        grid=(m // tm,),
        in_specs=[pl.BlockSpec((tm, d), lambda i: (i, 0)),
                  pl.BlockSpec((1, d), lambda i: (0, 0))],
        out_specs=pl.BlockSpec((tm, d), lambda i: (i, 0)),
        out_shape=jax.ShapeDtypeStruct((m, d), out_dtype),
        compiler_params=_cparams("parallel"),
        name="rmsnorm",
    )(x, g.reshape(1, d).astype(F32))


def _cast_kernel(x_ref, o_ref):
    o_ref[...] = x_ref[...].astype(o_ref.dtype)


def cast_layer_bf16(w, layer, n_cols, tr, tc):
    rows = w.shape[1]
    return pl.pallas_call(
        _cast_kernel,
        grid=(rows // tr, n_cols // tc),
        in_specs=[pl.BlockSpec((None, tr, tc), lambda i, j: (layer, i, j))],
        out_specs=pl.BlockSpec((tr, tc), lambda i, j: (i, j)),
        out_shape=jax.ShapeDtypeStruct((rows, n_cols), BF16),
        compiler_params=_cparams("parallel", "parallel"),
        name="weight_cast",
    )(w)


def _linear_kernel(x_ref, w_ref, o_ref):
    o_ref[...] = jnp.dot(x_ref[...], w_ref[...], preferred_element_type=F32).astype(o_ref.dtype)


def _linear_res_kernel(x_ref, w_ref, r_ref, o_ref):
    acc = jnp.dot(x_ref[...], w_ref[...], preferred_element_type=F32)
    o_ref[...] = (r_ref[...] + acc).astype(o_ref.dtype)


def linear(x, w, out_dtype, tm, tn, residual=None, name="linear"):
    m, k = x.shape
    n = w.shape[1]
    in_specs = [pl.BlockSpec((tm, k), lambda i, j: (i, 0)),
                pl.BlockSpec((k, tn), lambda i, j: (0, j))]
    args = [x, w]
    body = _linear_kernel
    if residual is not None:
        in_specs.append(pl.BlockSpec((tm, tn), lambda i, j: (i, j)))
        args.append(residual)
        body = _linear_res_kernel
    return pl.pallas_call(
        body,
        grid=(m // tm, n // tn),
        in_specs=in_specs,
        out_specs=pl.BlockSpec((tm, tn), lambda i, j: (i, j)),
        out_shape=jax.ShapeDtypeStruct((m, n), out_dtype),
        compiler_params=_cparams("parallel", "arbitrary"),
        name=name,
    )(*args)


def _ffn_up_kernel(h_ref, wg_ref, wu_ref, o_ref):
    h = h_ref[...]
    g = jnp.dot(h, wg_ref[...], preferred_element_type=F32)
    u = jnp.dot(h, wu_ref[...], preferred_element_type=F32)
    o_ref[...] = (g * jax.nn.sigmoid(g) * u).astype(o_ref.dtype)


def ffn_up(h, wg, wu, tm, tn):
    m, k = h.shape
    n = wg.shape[1]
    return pl.pallas_call(
        _ffn_up_kernel,
        grid=(m // tm, n // tn),
        in_specs=[pl.BlockSpec((tm, k), lambda i, j: (i, 0)),
                  pl.BlockSpec((k, tn), lambda i, j: (0, j)),
                  pl.BlockSpec((k, tn), lambda i, j: (0, j))],
        out_specs=pl.BlockSpec((tm, tn), lambda i, j: (i, j)),
        out_shape=jax.ShapeDtypeStruct((m, n), BF16),
        compiler_params=_cparams("parallel", "arbitrary"),
        name="ffn_up",
    )(h, wg, wu)


def _rope_kernel(x_ref, c_ref, s_ref, g_ref, o_ref, *, heads, half, norm, scale):
    c = c_ref[...]
    s = s_ref[...]
    lane = lax.broadcasted_iota(jnp.int32, c.shape, 1)
    first = (lane % (2 * half)) < half
    for h in range(heads):
        sl = slice(h * HEAD_DIM, (h + 1) * HEAD_DIM)
        x = x_ref[:, sl].astype(F32)
        if norm:
            ms = jnp.mean(x * x, axis=-1, keepdims=True)
            x = x * lax.rsqrt(ms + EPS) * g_ref[...]
        partner = jnp.where(first, pltpu.roll(x, HEAD_DIM - half, 1), pltpu.roll(x, half, 1))
        o_ref[:, sl] = ((x * c + partner * s) * scale).astype(o_ref.dtype)


def rope_prep(proj, col_off, heads, cos_t, sin_t, g, seq_len, *, half, norm, scale, tm=512):
    m = proj.shape[0]
    width = heads * HEAD_DIM
    nblk = seq_len // tm
    kern = functools.partial(_rope_kernel, heads=heads, half=half, norm=norm, scale=scale)
    return pl.pallas_call(
        kern,
        grid=(m // tm,),
        in_specs=[pl.BlockSpec((tm, width), lambda i: (i, col_off // width)),
                  pl.BlockSpec((tm, HEAD_DIM), lambda i: (i % nblk, 0)),
                  pl.BlockSpec((tm, HEAD_DIM), lambda i: (i % nblk, 0)),
                  pl.BlockSpec((1, HEAD_DIM), lambda i: (0, 0))],
        out_specs=pl.BlockSpec((tm, width), lambda i: (i, 0)),
        out_shape=jax.ShapeDtypeStruct((m, width), BF16),
        compiler_params=_cparams("parallel"),
        name="rope_prep",
    )(proj, cos_t, sin_t, g.reshape(1, HEAD_DIM).astype(F32))


def _lane_freqs(dims, theta, width):
    inv = jnp.float32(theta) ** (-jnp.arange(0, dims, 2, dtype=F32) / dims)
    pad = jnp.zeros((width - dims,), F32)
    ones = jnp.ones_like(inv)
    return jnp.concatenate([inv, inv, pad]), jnp.concatenate([-ones, ones, pad])


def rotary_tables(seq_len):
    pos = jnp.arange(seq_len, dtype=F32)[:, None]
    inv_a, sign_a = _lane_freqs(ROPE_DIMS, ROPE_THETA, HEAD_DIM)
    ang_a = pos * inv_a[None, :]
    half = HEAD_DIM // 2
    inv_h, sign_h = _lane_freqs(half, AXIAL_THETA, half)
    zeros = jnp.zeros((half,), F32)
    t = jnp.arange(seq_len, dtype=jnp.int32)
    row_pos = (t // GRID_W).astype(F32)[:, None]
    col_pos = (t % GRID_W).astype(F32)[:, None]
    ang_b = (row_pos * jnp.concatenate([inv_h, zeros])[None, :]
             + col_pos * jnp.concatenate([zeros, inv_h])[None, :])
    sign_b = jnp.concatenate([sign_h, sign_h])
    return (jnp.cos(ang_a), sign_a[None, :] * jnp.sin(ang_a),
            jnp.cos(ang_b), sign_b[None, :] * jnp.sin(ang_b))


DA_QBLK = 128
DA_HALF = 64


DA_ROWS = 256
DA_UNROLL = 16


def _dilated_kernel(q_ref, k_ref, v_ref, c_ref, s_ref, y_ref,
                    qf, kf, vf, qg, kg, vg, qc, kc, vc, oc, lc, o_run, l_run, bias,
                    *, seq_len, scale):
    n = seq_len
    half = ROPE_DIMS // 2
    lane = lax.broadcasted_iota(jnp.int32, (DA_ROWS, HEAD_DIM), 1)
    first = (lane % (2 * half)) < half

    q_i = lax.broadcasted_iota(jnp.int32, (DA_QBLK, 2 * DA_QBLK), 0)
    k_j = lax.broadcasted_iota(jnp.int32, (DA_QBLK, 2 * DA_QBLK), 1)
    for c in range(3):
        bias[c] = jnp.where(jnp.abs(k_j - q_i - c * DA_HALF) <= DA_HALF, 0.0, NEG)

    def rope(c, carry):
        rows = pl.ds(pl.multiple_of(c * DA_ROWS, DA_ROWS), DA_ROWS)
        cos, sin = c_ref[rows, :], s_ref[rows, :]
        for src, dst, mul in ((q_ref, qf, scale), (k_ref, kf, 1.0)):
            x = src[0, rows, :].astype(F32)
            partner = jnp.where(first, pltpu.roll(x, HEAD_DIM - half, 1), pltpu.roll(x, half, 1))
            dst[rows, :] = (x * cos + partner * sin) * mul
        vf[rows, :] = v_ref[0, rows, :].astype(F32)
        return carry

    lax.fori_loop(0, n // DA_ROWS, rope, 0)

    f32_sets = ((qf, kf, vf), (qg, kg, vg))
    assert len(DA_PATTERNS) <= len(f32_sets) + 1
    for pi, (_, dil) in enumerate(DA_PATTERNS):
        la = n // dil
        win = min(2 * DA_QBLK, la)
        per_class = la // DA_QBLK
        sub = min(DA_ROWS, la)

        def natural(r, c0, dil=dil, sub=sub):
            return pl.ds(r + dil * c0, sub, stride=dil) if dil > 1 else pl.ds(c0, sub)

        prev_dil = DA_PATTERNS[pi - 1][1] if pi else 1
        ratio = dil // prev_dil
        sources = f32_sets[max(pi - 1, 0)]
        keep = f32_sets[pi] if 0 < pi < len(DA_PATTERNS) - 1 else (None,) * 3
        for r in range(dil):
            for c0 in range(0, la, sub):
                start = (r % prev_dil) * (n // prev_dil) + r // prev_dil + ratio * c0
                src = pl.ds(start, sub, stride=ratio) if ratio > 1 else pl.ds(start, sub)
                dst = pl.ds(r * la + c0, sub)
                for x_src, x_keep, x_dst in zip(sources, keep, (qc, kc, vc)):
                    x = x_src[src, :]
                    if x_keep is not None:
                        x_keep[dst, :] = x
                    x_dst[dst, :] = x.astype(BF16)

        o_dst, l_dst = (o_run, l_run) if pi == 0 else (oc, lc)

        def band(g, carry, la=la, win=win, per_class=per_class, o_dst=o_dst, l_dst=l_dst):
            base = (g // per_class) * la
            a0 = (g % per_class) * DA_QBLK
            ws = jnp.clip(a0 - DA_HALF, 0, la - win)
            q_rows = pl.ds(pl.multiple_of(base + a0, DA_QBLK), DA_QBLK)
            k_rows = pl.ds(pl.multiple_of(base + ws, DA_HALF), win)
            s = lax.dot_general(qc[q_rows, :], kc[k_rows, :], (((1,), (1,)), ((), ())),
                                preferred_element_type=F32)
            s = s + bias[(a0 - ws) // DA_HALF, :, 0:win]
            m = jnp.max(s, axis=-1, keepdims=True)
            p = jnp.exp(s - m)
            l = jnp.sum(p, axis=-1, keepdims=True)
            o = jnp.dot(p.astype(BF16), vc[k_rows, :], preferred_element_type=F32)
            o_dst[q_rows, :] = o / l
            l_dst[q_rows, :] = jnp.broadcast_to(m + jnp.log(l), (DA_QBLK, HEAD_DIM))
            return carry

        lax.fori_loop(0, n // DA_QBLK, band, 0, unroll=DA_UNROLL)

        if pi == 0:
            continue
        for r in range(dil):
            for c0 in range(0, la, sub):
                nat = natural(r, c0)
                cls = pl.ds(r * la + c0, sub)
                l_old, l_new = l_run[nat, :], lc[cls, :]
                mx = jnp.maximum(l_old, l_new)
                w_old, w_new = jnp.exp(l_old - mx), jnp.exp(l_new - mx)
                tot = w_old + w_new
                o_run[nat, :] = (w_old * o_run[nat, :] + w_new * oc[cls, :]) / tot
                l_run[nat, :] = mx + jnp.log(tot)

    def emit(c, carry):
        rows = pl.ds(pl.multiple_of(c * DA_ROWS, DA_ROWS), DA_ROWS)
        y_ref[0, rows, :] = o_run[rows, :].astype(y_ref.dtype)
        return carry

    lax.fori_loop(0, n // DA_ROWS, emit, 0)


def dilated_attention(proj, cos_t, sin_t, batch, seq_len):
    pv = proj.reshape(batch, seq_len, proj.shape[-1])
    blk = (1, seq_len, HEAD_DIM)

    def col(off):
        return lambda b, h: (b, 0, off // HEAD_DIM + h)

    table = pl.BlockSpec((seq_len, HEAD_DIM), lambda b, h: (0, 0))
    f32_rows = pltpu.VMEM((seq_len, HEAD_DIM), F32)
    bf16_rows = pltpu.VMEM((seq_len, HEAD_DIM), BF16)
    y = pl.pallas_call(
        functools.partial(_dilated_kernel, seq_len=seq_len, scale=HEAD_DIM ** -0.5),
        grid=(batch, DA_HEADS),
        in_specs=[pl.BlockSpec(blk, col(OFF_AQ)), pl.BlockSpec(blk, col(OFF_AK)),
                  pl.BlockSpec(blk, col(OFF_AV)), table, table],
        out_specs=pl.BlockSpec(blk, lambda b, h: (b, 0, h)),
        out_shape=jax.ShapeDtypeStruct((batch, seq_len, GROUP_WIDTH), BF16),
        scratch_shapes=[f32_rows] * 6 + [bf16_rows] * 3 + [f32_rows] * 4
        + [pltpu.VMEM((3, DA_QBLK, 2 * DA_QBLK), F32)],
        compiler_params=_cparams("parallel", "parallel"),
        name="dilated_attention",
    )(pv, pv, pv, cos_t, sin_t)
    return y.reshape(batch * seq_len, GROUP_WIDTH)


GQA_SCORE_ELEMS = 1024 * 1024
LOG2E = 1.4426950408889634


def _gqa_kernel(q_ref, k_ref, v_ref, o_ref):
    k = k_ref[0]
    v = v_ref[0]
    for g in range(GQA_GROUP):
        sl = slice(g * HEAD_DIM, (g + 1) * HEAD_DIM)
        s = lax.dot_general(q_ref[0, :, sl], k, (((1,), (1,)), ((), ())),
                            preferred_element_type=F32)
        m = jnp.max(s, axis=-1, keepdims=True)
        p = jnp.exp2(s - m)
        l = jnp.sum(p, axis=-1, keepdims=True)
        o = jnp.dot(p.astype(BF16), v, preferred_element_type=F32)
        o_ref[0, :, sl] = (o / l).astype(o_ref.dtype)


def gqa_attention(q, k, proj, batch, seq_len):
    tq = min(seq_len, GQA_SCORE_ELEMS // seq_len)
    qv = q.reshape(batch, seq_len, GQA_HEADS * HEAD_DIM)
    kv = k.reshape(batch, seq_len, GQA_KV_HEADS * HEAD_DIM)
    pv = proj.reshape(batch, seq_len, proj.shape[-1])
    gw = GQA_GROUP * HEAD_DIM
    v_blk0 = OFF_BV // HEAD_DIM
    y = pl.pallas_call(
        _gqa_kernel,
        grid=(batch, GQA_KV_HEADS, seq_len // tq),
        in_specs=[pl.BlockSpec((1, tq, gw), lambda b, h, i: (b, i, h)),
                  pl.BlockSpec((1, seq_len, HEAD_DIM), lambda b, h, i: (b, 0, h)),
                  pl.BlockSpec((1, seq_len, HEAD_DIM), lambda b, h, i: (b, 0, v_blk0 + h))],
        out_specs=pl.BlockSpec((1, tq, gw), lambda b, h, i: (b, i, h)),
        out_shape=jax.ShapeDtypeStruct((batch, seq_len, GQA_HEADS * HEAD_DIM), BF16),
        compiler_params=_cparams("parallel", "parallel", "arbitrary"),
        name="gqa_attention",
    )(qv, kv, pv)
    return y.reshape(batch * seq_len, GQA_HEADS * HEAD_DIM)


DFT_TM = 512
DFT_TN = 512
DFT_COLS = 256


HY_TC = 512


def _short_conv_kernel(u_ref, w_ref, b_ref, oe_ref, oo_ref, uf_ref):
    n = u_ref.shape[1]
    hn = n // 2
    row = lax.broadcasted_iota(jnp.int32, (hn, LANES), 0)
    for c in range(HY_TC // LANES):
        cs = slice(c * LANES, (c + 1) * LANES)
        uf_ref[c] = u_ref[0, :, cs].astype(F32)
        ue = uf_ref[c, pl.ds(0, hn, stride=2), :]
        uo = uf_ref[c, pl.ds(1, hn, stride=2), :]
        uo_prev = jnp.where(row == 0, 0.0, pltpu.roll(uo, 1, 0))
        ue_next = jnp.where(row == hn - 1, 0.0, pltpu.roll(ue, hn - 1, 0))
        w0, w1, w2, b = w_ref[0:1, cs], w_ref[1:2, cs], w_ref[2:3, cs], b_ref[:, cs]
        oe_ref[0, :, cs] = b + uo_prev * w0 + ue * w1 + uo * w2
        oo_ref[0, :, cs] = b + ue * w0 + uo * w1 + ue_next * w2


def short_conv(proj, conv_w, conv_b, batch, seq_len):
    pv = proj.reshape(batch, seq_len, proj.shape[-1])
    width = conv_w.shape[-1]
    blk0 = OFF_CU // HY_TC
    out = jax.ShapeDtypeStruct((batch, seq_len // 2, width), F32)
    out_spec = pl.BlockSpec((1, seq_len // 2, HY_TC), lambda b, j: (b, 0, j))
    return pl.pallas_call(
        _short_conv_kernel,
        grid=(batch, width // HY_TC),
        in_specs=[pl.BlockSpec((1, seq_len, HY_TC), lambda b, j: (b, 0, blk0 + j)),
                  pl.BlockSpec((3, HY_TC), lambda b, j: (0, j)),
                  pl.BlockSpec((1, HY_TC), lambda b, j: (0, j))],
        out_specs=[out_spec, out_spec],
        out_shape=[out, out],
        scratch_shapes=[pltpu.VMEM((HY_TC // LANES, seq_len, LANES), F32)],
        compiler_params=_cparams("parallel", "parallel"),
        name="hyena_short_conv",
    )(pv, conv_w.astype(F32), conv_b.reshape(1, width).astype(F32))


def _interleave_kernel(e_ref, o_ref, y_ref, buf_ref):
    hn = e_ref.shape[1]
    for c in range(HY_TC // LANES):
        cs = slice(c * LANES, (c + 1) * LANES)
        buf_ref[c, pl.ds(0, hn, stride=2), :] = e_ref[0, :, cs]
        buf_ref[c, pl.ds(1, hn, stride=2), :] = o_ref[0, :, cs]
        y_ref[0, :, cs] = buf_ref[c].astype(y_ref.dtype)


def interleave(even, odd, out_dtype):
    batch, hn, width = even.shape
    in_spec = pl.BlockSpec((1, hn, HY_TC), lambda b, j: (b, 0, j))
    return pl.pallas_call(
        _interleave_kernel,
        grid=(batch, width // HY_TC),
        in_specs=[in_spec, in_spec],
        out_specs=pl.BlockSpec((1, 2 * hn, HY_TC), lambda b, j: (b, 0, j)),
        out_shape=jax.ShapeDtypeStruct((batch, 2 * hn, width), out_dtype),
        scratch_shapes=[pltpu.VMEM((HY_TC // LANES, 2 * hn, LANES), F32)],
        compiler_params=_cparams("parallel", "parallel"),
        name="hyena_interleave",
    )(even, odd)


def dft_matrices(seq_len):
    n = seq_len
    assert n % 4 == 0
    half = DFT_TM // 2
    i = jnp.arange(n, dtype=jnp.int32)
    tile, w = i // DFT_TM, i % DFT_TM
    imag = (w // half) == 1
    odd = 2 * (tile * half + w % half) + 1
    unit = math.pi / (2 * n)
    fwd = []
    for parity in range(2):
        s = jnp.arange(parity, n, 2, dtype=jnp.int32)
        ang = ((odd[:, None] * s[None, :]) % (4 * n)).astype(F32) * unit
        fwd.append(jnp.where(imag[:, None], -jnp.sin(ang), jnp.cos(ang)).astype(BF16))
    fwd = lax.optimization_barrier(fwd)
    f = jnp.arange(n // 2, dtype=jnp.int32)
    fold = []
    for eighths in ((2 * f + 1) % 8, (2 * (n - 1 - f) + 1) % 8):
        ang = eighths.astype(F32) * (math.pi / 4)
        fold += [(jnp.cos(ang) / n)[:, None], (jnp.sin(ang) / n)[:, None]]
    return fwd[0], fwd[1], fwd[0].T, fwd[1].T, tuple(fold)


def hyena_filters(seq_len, parity, w1, b1, fr1, w2, b2, fr2, w3, decay):
    hp = lax.Precision.HIGHEST
    n = jnp.arange(parity, seq_len, 2, dtype=F32)
    t = n / (seq_len - 1)
    f = jnp.linspace(1e-4, HY_BANDS - 1, HY_BANDS, dtype=F32)
    ang = (2.0 * math.pi / seq_len) * n[:, None] * f[None, :]
    z = jnp.concatenate([t[:, None], jnp.cos(ang), -jnp.sin(ang)], axis=-1)
    h = jnp.sin(fr1 * (jnp.dot(z, w1, precision=hp) + b1))
    h = jnp.sin(fr2 * (jnp.dot(h, w2, precision=hp) + b2))
    h = jnp.dot(h, w3, precision=hp)
    r = jnp.abs(n - seq_len // 2) / (seq_len // 2)
    return h * jnp.exp(-r[:, None] * decay[None, :])


def _cmul(ar, ai, br, bi):
    return ar * br - ai * bi, ar * bi + ai * br


def _dft_fwd_kernel(*refs, with_filter):
    if with_filter:
        fe_ref, fo_ref, ze_ref, zo_ref, h_ref, o_ref, zbe_ref, zbo_ref = refs
    else:
        fe_ref, fo_ref, ze_ref, zo_ref, c1r, c1i, c2r, c2i, o_ref, zbe_ref, zbo_ref = refs
    tn = ze_ref.shape[2]
    half = DFT_TM // 2

    @pl.when(pl.program_id(2) == 0)
    def _():
        zbe_ref[...] = ze_ref[0].astype(BF16)
        zbo_ref[...] = zo_ref[0].astype(BF16)

    for c0 in range(0, tn, DFT_COLS):
        cs = slice(c0, c0 + DFT_COLS)
        qo = slice(tn + c0, tn + c0 + DFT_COLS)
        a = jnp.dot(fe_ref[...], zbe_ref[:, cs], preferred_element_type=F32)
        b = jnp.dot(fo_ref[...], zbo_ref[:, cs], preferred_element_type=F32)
        u1r, u1i = a[:half] + b[:half], a[half:] + b[half:]
        u2r, u2i = a[:half] - b[:half], b[half:] - a[half:]
        if not with_filter:
            u1r, u1i = _cmul(u1r, u1i, c1r[...], c1i[...])
            u2r, u2i = _cmul(u2r, u2i, c2r[...], c2i[...])
            for q, val in enumerate((u1r, u1i, u2r, u2i)):
                o_ref[0, q * half:(q + 1) * half, cs] = val
            continue
        p1r, p1i = _cmul(u1r, u1i, h_ref[0:half, cs], h_ref[half:2 * half, cs])
        p2r, p2i = _cmul(u2r, u2i, h_ref[2 * half:3 * half, cs], h_ref[3 * half:, cs])
        o_ref[0, :half, cs] = (p1r + p2r).astype(o_ref.dtype)
        o_ref[0, half:, cs] = (p1i - p2i).astype(o_ref.dtype)
        o_ref[0, :half, qo] = (p1r - p2r).astype(o_ref.dtype)
        o_ref[0, half:, qo] = (p1i + p2i).astype(o_ref.dtype)


def dft_forward(tables, z_pair, z_blk0=0, spec=None, spec_blk0=0):
    batch, hn, _ = z_pair[0].shape
    seq_len = 2 * hn
    n_ch = HY_CH if spec is not None else z_pair[0].shape[2]
    f_spec = pl.BlockSpec((DFT_TM, hn), lambda b, j, i: (i, 0))
    z_spec = pl.BlockSpec((1, hn, DFT_TN), lambda b, j, i: (b, 0, z_blk0 + j))
    in_specs = [f_spec, f_spec, z_spec, z_spec]
    args = [tables[0], tables[1], z_pair[0], z_pair[1]]
    if spec is not None:
        in_specs.append(pl.BlockSpec((2 * DFT_TM, DFT_TN), lambda b, j, i: (i, spec_blk0 + j)))
        args.append(spec)
        out_spec = pl.BlockSpec((1, DFT_TM, 2 * DFT_TN), lambda b, j, i: (b, i, j))
        out_shape = jax.ShapeDtypeStruct((batch, seq_len, 2 * n_ch), BF16)
    else:
        in_specs += [pl.BlockSpec((DFT_TM // 2, 1), lambda b, j, i: (i, 0))] * 4
        args += list(tables[4])
        out_spec = pl.BlockSpec((1, 2 * DFT_TM, DFT_TN), lambda b, j, i: (b, i, j))
        out_shape = jax.ShapeDtypeStruct((batch, 2 * seq_len, n_ch), F32)
    return pl.pallas_call(
        functools.partial(_dft_fwd_kernel, with_filter=spec is not None),
        grid=(batch, n_ch // DFT_TN, seq_len // DFT_TM),
        in_specs=in_specs,
        out_specs=out_spec,
        out_shape=out_shape,
        scratch_shapes=[pltpu.VMEM((hn, DFT_TN), BF16), pltpu.VMEM((hn, DFT_TN), BF16)],
        compiler_params=_cparams("parallel", "parallel", "arbitrary"),
        name="hyena_dft_fwd" if spec is not None else "hyena_filter_dft",
    )(*args)


def _dft_inv_kernel(ge_ref, go_ref, q_ref, ze_ref, zo_ref, xe_ref, xo_ref, sk_ref,
                    oe_ref, oo_ref):
    tn = ze_ref.shape[2]
    sk = sk_ref[...]
    ye = jnp.dot(ge_ref[...], q_ref[0, :, :tn], preferred_element_type=F32)
    oe_ref[0] = xe_ref[0] * (ye + sk * ze_ref[0])
    yo = jnp.dot(go_ref[...], q_ref[0, :, tn:], preferred_element_type=F32)
    oo_ref[0] = xo_ref[0] * (yo + sk * zo_ref[0])


def dft_inverse(tables, q, z_pair, z_blk0, gate_pair, gate_blk0, skip, skip_blk0, tm=256):
    batch, seq_len, _ = q.shape
    hn = seq_len // 2
    g_spec = pl.BlockSpec((tm, seq_len), lambda b, j, i: (i, 0))
    z_spec = pl.BlockSpec((1, tm, DFT_TN), lambda b, j, i: (b, i, z_blk0 + j))
    x_spec = pl.BlockSpec((1, tm, DFT_TN), lambda b, j, i: (b, i, gate_blk0 + j))
    out_spec = pl.BlockSpec((1, tm, DFT_TN), lambda b, j, i: (b, i, j))
    out = jax.ShapeDtypeStruct((batch, hn, HY_CH), F32)
    return pl.pallas_call(
        _dft_inv_kernel,
        grid=(batch, HY_CH // DFT_TN, hn // tm),
        in_specs=[g_spec, g_spec,
                  pl.BlockSpec((1, seq_len, 2 * DFT_TN), lambda b, j, i: (b, 0, j)),
                  z_spec, z_spec, x_spec, x_spec,
                  pl.BlockSpec((1, DFT_TN), lambda b, j, i: (0, skip_blk0 + j))],
        out_specs=[out_spec, out_spec],
        out_shape=[out, out],
        compiler_params=_cparams("parallel", "parallel", "arbitrary"),
        name="hyena_dft_inv",
    )(tables[2], tables[3], q, z_pair[0], z_pair[1], gate_pair[0], gate_pair[1], skip)


def hyena_mixer(proj, batch, seq_len, tables, spec, conv_w, conv_b, skip):
    uc = short_conv(proj, conv_w, conv_b, batch, seq_len)
    nb = HY_CH // DFT_TN
    sk = skip.reshape(1, HY_ORDER * HY_CH).astype(F32)
    q = dft_forward(tables, uc, 0, spec, 0)
    z = dft_inverse(tables, q, uc, 0, uc, nb, sk, 0)
    q = dft_forward(tables, z, 0, spec, nb)
    y = dft_inverse(tables, q, z, 0, uc, 2 * nb, sk, nb)
    return interleave(y[0], y[1], BF16).reshape(batch * seq_len, HY_CH)


ML_CHUNK = 256
ML_GROWS = 8


def _split3_dot(x, tri):
    hi = x.astype(BF16)
    r1 = x - hi.astype(F32)
    mid = r1.astype(BF16)
    lo = (r1 - mid.astype(F32)).astype(BF16)
    return (jnp.dot(hi, tri, preferred_element_type=F32)
            + jnp.dot(mid, tri, preferred_element_type=F32)
            + jnp.dot(lo, tri, preferred_element_type=F32))


def _log_sigmoid(x):
    return jnp.minimum(x, 0.0) - jnp.log1p(jnp.exp(-jnp.abs(x)))


def _mlstm_kernel(q_ref, k_ref, v_ref, o_ref, g_ref, gb_ref, ng_ref, y_ref,
                  hf_ref, hb_ref, cf_ref, nf_ref, mf_ref, cb_ref, nb_ref, mb_ref, *, seq_len):
    lc = ML_CHUNK
    nc = seq_len // lc
    for ref in (cf_ref, nf_ref, mf_ref, cb_ref, nb_ref, mb_ref):
        ref[...] = jnp.zeros(ref.shape, ref.dtype)

    t_i = lax.broadcasted_iota(jnp.int32, (lc, lc), 0)
    s_i = lax.broadcasted_iota(jnp.int32, (lc, lc), 1)
    eye = t_i == s_i
    lower = s_i <= t_i
    upper = s_i >= t_i
    lower_b = lower.astype(BF16)
    upper_b = upper.astype(BF16)
    k_scale = ML_HEAD_DIM ** -0.5

    def to_col(row):
        return jnp.sum(jnp.where(eye, row, 0.0), axis=1, keepdims=True)

    def chunk(c, reverse, c_ref, n_ref, m_ref, h_ref):
        t0 = pl.multiple_of(c * lc, lc)
        q = q_ref[0, pl.ds(t0, lc), :]
        k = k_ref[0, pl.ds(t0, lc), :] * k_scale
        v = v_ref[0, pl.ds(t0, lc), :]
        g = g_ref[:, pl.ds(t0, lc)] + gb_ref[0]
        r0 = 2 if reverse else 0
        li_row = g[r0:r0 + 1, :]
        lf_row = _log_sigmoid(g[r0 + 1:r0 + 2, :])
        b_row = _split3_dot(jnp.broadcast_to(lf_row, (ML_GROWS, lc)),
                            lower_b if reverse else upper_b)[0:1, :]
        seen = upper if reverse else lower
        b_col = to_col(b_row)
        li_col = to_col(li_row)
        m_old = m_ref[...]
        dm = jnp.where(seen, b_col - b_row + li_row, NEG)
        inter = b_col + m_old
        m_t = jnp.maximum(inter, jnp.max(dm, axis=1, keepdims=True))
        wm = jnp.exp(dm - m_t)
        a_inter = jnp.exp(inter - m_t)
        qk = lax.dot_general(q, k, (((1,), (1,)), ((), ())), preferred_element_type=F32)
        s = qk * wm
        c_old = c_ref[...]
        n_old = n_ref[...]
        num = (a_inter * jnp.dot(q, c_old.astype(BF16), preferred_element_type=F32)
               + jnp.dot(s.astype(BF16), v, preferred_element_type=F32))
        qn = jnp.sum(q.astype(F32) * n_old, axis=1, keepdims=True)
        den = a_inter * qn + jnp.sum(s, axis=1, keepdims=True)
        h_ref[pl.ds(t0, lc), :] = num / jnp.maximum(jnp.abs(den), jnp.exp(-m_t))
        b_last = b_row[:, 0:1] if reverse else b_row[:, lc - 1:lc]
        g_row = b_last - b_row + li_row
        g_col = b_last - b_col + li_col
        m_new = jnp.maximum(b_last + m_old, jnp.max(g_row, axis=1, keepdims=True))
        a_old = jnp.exp(b_last + m_old - m_new)
        kw = k.astype(F32) * jnp.exp(g_col - m_new)
        c_ref[...] = a_old * c_old + lax.dot_general(
            kw.astype(BF16), v, (((0,), (0,)), ((), ())), preferred_element_type=F32)
        n_ref[...] = a_old * n_old + jnp.sum(kw, axis=0, keepdims=True)
        m_ref[...] = m_new

    def step(c, carry):
        chunk(c, False, cf_ref, nf_ref, mf_ref, hf_ref)
        chunk(nc - 1 - c, True, cb_ref, nb_ref, mb_ref, hb_ref)
        return carry

    lax.fori_loop(0, nc, step, 0)

    def finish(c, carry):
        t0 = pl.multiple_of(c * lc, lc)
        hs = hf_ref[pl.ds(t0, lc), :] + hb_ref[pl.ds(t0, lc), :]
        ms = jnp.mean(hs * hs, axis=-1, keepdims=True)
        hs = hs * lax.rsqrt(ms + EPS) * ng_ref[...]
        gate = jax.nn.sigmoid(o_ref[0, pl.ds(t0, lc), :].astype(F32))
        y_ref[0, pl.ds(t0, lc), :] = (hs * gate).astype(y_ref.dtype)
        return carry

    lax.fori_loop(0, nc, finish, 0)


def _gate_proj_kernel(w_ref, h_ref, o_ref):
    o_ref[...] = lax.dot_general(w_ref[...], h_ref[...], (((1,), (1,)), ((), ())),
                                 preferred_element_type=F32)


def gate_rows_weight(w_gate_cols):
    d = w_gate_cols.shape[0]
    w = w_gate_cols.T.reshape(4, ML_HEADS, d).transpose(1, 0, 2)
    w = jnp.pad(w, ((0, 0), (0, ML_GROWS - 4), (0, 0)))
    return w.reshape(ML_HEADS * ML_GROWS, d).astype(BF16)


def gate_projection(h, w_rows, tm=1024):
    m, d = h.shape
    rows = w_rows.shape[0]
    return pl.pallas_call(
        _gate_proj_kernel,
        grid=(m // tm,),
        in_specs=[pl.BlockSpec((rows, d), lambda i: (0, 0)),
                  pl.BlockSpec((tm, d), lambda i: (i, 0))],
        out_specs=pl.BlockSpec((rows, tm), lambda i: (0, i)),
        out_shape=jax.ShapeDtypeStruct((rows, m), F32),
        compiler_params=_cparams("parallel"),
        name="in_proj_gates",
    )(w_rows, h)


def mlstm_mixer(proj, gates_t, gate_b, norm_g, batch, seq_len):
    d = ML_HEAD_DIM
    pv = proj.reshape(batch, seq_len, proj.shape[-1])
    gb = jnp.pad(gate_b.astype(F32).T, ((0, 0), (0, ML_GROWS - 4)))[..., None]
    blk = (1, seq_len, d)

    def col(off):
        return lambda b, h: (b, 0, off // d + h)

    y = pl.pallas_call(
        functools.partial(_mlstm_kernel, seq_len=seq_len),
        grid=(batch, ML_HEADS),
        in_specs=[pl.BlockSpec(blk, col(OFF_DQ)),
                  pl.BlockSpec(blk, col(OFF_DK)),
                  pl.BlockSpec(blk, col(OFF_DV)),
                  pl.BlockSpec(blk, col(OFF_DO)),
                  pl.BlockSpec((ML_GROWS, seq_len), lambda b, h: (h, b)),
                  pl.BlockSpec((1, ML_GROWS, 1), lambda b, h: (h, 0, 0)),
                  pl.BlockSpec((1, d), lambda b, h: (0, h))],
        out_specs=pl.BlockSpec(blk, lambda b, h: (b, 0, h)),
        out_shape=jax.ShapeDtypeStruct((batch, seq_len, ML_HEADS * d), BF16),
        scratch_shapes=[pltpu.VMEM((seq_len, d), F32), pltpu.VMEM((seq_len, d), F32),
                        pltpu.VMEM((d, d), F32), pltpu.VMEM((1, d), F32), pltpu.VMEM((1, 1), F32),
                        pltpu.VMEM((d, d), F32), pltpu.VMEM((1, d), F32), pltpu.VMEM((1, 1), F32)],
        compiler_params=_cparams("parallel", "parallel"),
        name="mlstm",
    )(pv, pv, pv, pv, gates_t, gb, norm_g.reshape(1, ML_HEADS * d).astype(F32))
    return y.reshape(batch * seq_len, ML_HEADS * d)


def _trunk(x, wts, shared):
    batch, seq_len, d = x.shape
    m = batch * seq_len
    ca, sa, cb, sb = shared[:4]
    dft = shared[4:]
    q_scale = HEAD_DIM ** -0.5
    x2 = x.reshape(m, d)
    for l in range(DEPTH):
        w = wts[l]
        h = rmsnorm(x2, w["norm1_g"], BF16)
        proj = linear(h, w["w_in_main"], BF16, 2048, 512, name="in_proj")
        gates = gate_projection(h, w["w_in_gate"])
        y_a = dilated_attention(proj, ca, sa, batch, seq_len)
        b_q = rope_prep(proj, OFF_BQ, GQA_HEADS, cb, sb, w["qk_norm_g"][0], seq_len,
                        half=HEAD_DIM // 4, norm=True, scale=q_scale * LOG2E)
        b_k = rope_prep(proj, OFF_BK, GQA_KV_HEADS, cb, sb, w["qk_norm_g"][1], seq_len,
                        half=HEAD_DIM // 4, norm=True, scale=1.0)
        y_b = gqa_attention(b_q, b_k, proj, batch, seq_len)
        filt = [hyena_filters(seq_len, parity, w["hy_w1"], w["hy_b1"], w["hy_freq1"], w["hy_w2"],
                              w["hy_b2"], w["hy_freq2"], w["hy_w3"], w["hy_decay"])[None]
                for parity in range(2)]
        spec = dft_forward(dft, filt)[0]
        y_c = hyena_mixer(proj, batch, seq_len, dft, spec,
                          w["hy_conv_w"], w["hy_conv_b"], w["hy_skip"])
        y_d = mlstm_mixer(proj, gates, w["ml_gate_b"], w["ml_norm_g"], batch, seq_len)
        y = jnp.concatenate([y_a, y_b, y_c, y_d], axis=-1)
        x2 = linear(y, w["w_out"], F32, 1024, 512, residual=x2, name="out_proj")
        h = rmsnorm(x2, w["norm2_g"], BF16)
        act = ffn_up(h, w["w_gate"], w["w_up"], 2048, 256)
        x2 = linear(act, w["w_down"], F32, 512, 512, residual=x2, name="ffn_down")
    return rmsnorm(x2, shared_final_g(wts), F32).reshape(batch, seq_len, d)


def shared_final_g(wts):
    return wts[0]["final_g"]


def kernel(x_prompt, x_sample, norm1_g, w_in, qk_norm_g, hy_conv_w, hy_conv_b, hy_w1, hy_b1,
           hy_freq1, hy_w2, hy_b2, hy_freq2, hy_w3, hy_decay, hy_skip, ml_gate_b, ml_norm_g,
           w_out, norm2_g, w_gate, w_up, w_down, final_g):
    wts = []
    d_rows = D_MODEL
    for l in range(DEPTH):
        wts.append(dict(
            norm1_g=norm1_g[l], w_in_main=cast_layer_bf16(w_in, l, N_MAIN, d_rows, 512),
            w_in_gate=gate_rows_weight(w_in[l, :, N_MAIN:]), qk_norm_g=qk_norm_g[l],
            hy_conv_w=hy_conv_w[l], hy_conv_b=hy_conv_b[l], hy_w1=hy_w1[l], hy_b1=hy_b1[l],
            hy_freq1=hy_freq1[l], hy_w2=hy_w2[l], hy_b2=hy_b2[l], hy_freq2=hy_freq2[l],
            hy_w3=hy_w3[l], hy_decay=hy_decay[l], hy_skip=hy_skip[l], ml_gate_b=ml_gate_b[l],
            ml_norm_g=ml_norm_g[l], w_out=cast_layer_bf16(w_out, l, D_MODEL, d_rows, 512),
            norm2_g=norm2_g[l],
            w_gate=cast_layer_bf16(w_gate, l, FF_DIM, d_rows, 256),
            w_up=cast_layer_bf16(w_up, l, FF_DIM, d_rows, 256),
            w_down=cast_layer_bf16(w_down, l, D_MODEL, FF_DIM // 4, 1024), final_g=final_g))
    outs = []
    for x in (x_prompt, x_sample):
        seq_len = x.shape[1]
        shared = rotary_tables(seq_len) + dft_matrices(seq_len)
        outs.append(_trunk(x, wts, shared))
    return tuple(outs)
```

```python
import functools
import math

import jax
import jax.numpy as jnp
from jax import lax
from jax.experimental import pallas as pl
from jax.experimental.pallas import tpu as pltpu

F32 = jnp.float32
BF16 = jnp.bfloat16

D_MODEL = 4096
DEPTH = 2
GROUP_WIDTH = 1024
HEAD_DIM = 128
DA_HEADS = 8
DA_PATTERNS = ((128, 1), (512, 4), (2048, 16))
ROPE_THETA = 500000.0
ROPE_DIMS = 32
GQA_HEADS = 8
GQA_KV_HEADS = 2
GQA_GROUP = GQA_HEADS // GQA_KV_HEADS
AXIAL_THETA = 10000.0
GRID_W = 64
HY_CH = 1024
HY_ORDER = 2
HY_EMB = 33
HY_BANDS = 16
ML_HEADS = 4
ML_HEAD_DIM = 256
FF_DIM = 11008
EPS = 1e-6
N_MAIN = 11776
N_GATE = 4 * ML_HEADS

OFF_AQ, OFF_AK, OFF_AV = 0, 1024, 2048
OFF_BQ, OFF_BK, OFF_BV = 3072, 4096, 4352
OFF_CU = 4608
OFF_DQ, OFF_DK, OFF_DV, OFF_DO = 7680, 8704, 9728, 10752

LANES = 128
NEG = -1e30
VMEM_LIMIT = 56 * 1024 * 1024


def _cparams(*sem):
    return pltpu.CompilerParams(dimension_semantics=sem, vmem_limit_bytes=VMEM_LIMIT)


def _rmsnorm_kernel(x_ref, g_ref, o_ref):
    x = x_ref[...].astype(F32)
    ms = jnp.mean(x * x, axis=-1, keepdims=True)
    o_ref[...] = (x * lax.rsqrt(ms + EPS) * g_ref[...]).astype(o_ref.dtype)


def rmsnorm(x, g, out_dtype, tm=256):
    m, d = x.shape
    return pl.pallas_call(
        _rmsnorm_kernel,
        grid=(m // tm,),
        in_specs=[pl.BlockSpec((tm, d), lambda i: (i, 0)),
                  pl.BlockSpec((1, d), lambda i: (0, 0))],
        out_specs=pl.BlockSpec((tm, d), lambda i: (i, 0)),
        out_shape=jax.ShapeDtypeStruct((m, d), out_dtype),
        compiler_params=_cparams("parallel"),
        name="rmsnorm",
    )(x, g.reshape(1, d).astype(F32))


def _cast_kernel(x_ref, o_ref):
    o_ref[...] = x_ref[...].astype(o_ref.dtype)


def cast_layer_bf16(w, layer, n_cols, tr, tc):
    rows = w.shape[1]
    return pl.pallas_call(
        _cast_kernel,
        grid=(rows // tr, n_cols // tc),
        in_specs=[pl.BlockSpec((None, tr, tc), lambda i, j: (layer, i, j))],
        out_specs=pl.BlockSpec((tr, tc), lambda i, j: (i, j)),
        out_shape=jax.ShapeDtypeStruct((rows, n_cols), BF16),
        compiler_params=_cparams("parallel", "parallel"),
        name="weight_cast",
    )(w)


def _linear_kernel(x_ref, w_ref, o_ref):
    o_ref[...] = jnp.dot(x_ref[...], w_ref[...], preferred_element_type=F32).astype(o_ref.dtype)


def _linear_res_kernel(x_ref, w_ref, r_ref, o_ref):
    acc = jnp.dot(x_ref[...], w_ref[...], preferred_element_type=F32)
    o_ref[...] = (r_ref[...] + acc).astype(o_ref.dtype)


def linear(x, w, out_dtype, tm, tn, residual=None, name="linear"):
    m, k = x.shape
    n = w.shape[1]
    in_specs = [pl.BlockSpec((tm, k), lambda i, j: (i, 0)),
                pl.BlockSpec((k, tn), lambda i, j: (0, j))]
    args = [x, w]
    body = _linear_kernel
    if residual is not None:
        in_specs.append(pl.BlockSpec((tm, tn), lambda i, j: (i, j)))
        args.append(residual)
        body = _linear_res_kernel
    return pl.pallas_call(
        body,
        grid=(m // tm, n // tn),
        in_specs=in_specs,
        out_specs=pl.BlockSpec((tm, tn), lambda i, j: (i, j)),
        out_shape=jax.ShapeDtypeStruct((m, n), out_dtype),
        compiler_params=_cparams("parallel", "arbitrary"),
        name=name,
    )(*args)


def _ffn_up_kernel(h_ref, wg_ref, wu_ref, o_ref):
    h = h_ref[...]
    g = jnp.dot(h, wg_ref[...], preferred_element_type=F32)
    u = jnp.dot(h, wu_ref[...], preferred_element_type=F32)
    o_ref[...] = (g * jax.nn.sigmoid(g) * u).astype(o_ref.dtype)


def ffn_up(h, wg, wu, tm, tn):
    m, k = h.shape
    n = wg.shape[1]
    return pl.pallas_call(
        _ffn_up_kernel,
        grid=(m // tm, n // tn),
        in_specs=[pl.BlockSpec((tm, k), lambda i, j: (i, 0)),
                  pl.BlockSpec((k, tn), lambda i, j: (0, j)),
                  pl.BlockSpec((k, tn), lambda i, j: (0, j))],
        out_specs=pl.BlockSpec((tm, tn), lambda i, j: (i, j)),
        out_shape=jax.ShapeDtypeStruct((m, n), BF16),
        compiler_params=_cparams("parallel", "arbitrary"),
        name="ffn_up",
    )(h, wg, wu)


def _rope_kernel(x_ref, c_ref, s_ref, g_ref, o_ref, *, heads, half, norm, scale):
    c = c_ref[...]
    s = s_ref[...]
    lane = lax.broadcasted_iota(jnp.int32, c.shape, 1)
    first = (lane % (2 * half)) < half
    for h in range(heads):
        sl = slice(h * HEAD_DIM, (h + 1) * HEAD_DIM)
        x = x_ref[:, sl].astype(F32)
        if norm:
            ms = jnp.mean(x * x, axis=-1, keepdims=True)
            x = x * lax.rsqrt(ms + EPS) * g_ref[...]
        partner = jnp.where(first, pltpu.roll(x, HEAD_DIM - half, 1), pltpu.roll(x, half, 1))
        o_ref[:, sl] = ((x * c + partner * s) * scale).astype(o_ref.dtype)


def rope_prep(proj, col_off, heads, cos_t, sin_t, g, seq_len, *, half, norm, scale, tm=512):
    m = proj.shape[0]
    width = heads * HEAD_DIM
    nblk = seq_len // tm
    kern = functools.partial(_rope_kernel, heads=heads, half=half, norm=norm, scale=scale)
    return pl.pallas_call(
        kern,
        grid=(m // tm,),
        in_specs=[pl.BlockSpec((tm, width), lambda i: (i, col_off // width)),
                  pl.BlockSpec((tm, HEAD_DIM), lambda i: (i % nblk, 0)),
                  pl.BlockSpec((tm, HEAD_DIM), lambda i: (i % nblk, 0)),
                  pl.BlockSpec((1, HEAD_DIM), lambda i: (0, 0))],
        out_specs=pl.BlockSpec((tm, width), lambda i: (i, 0)),
        out_shape=jax.ShapeDtypeStruct((m, width), BF16),
        compiler_params=_cparams("parallel"),
        name="rope_prep",
    )(proj, cos_t, sin_t, g.reshape(1, HEAD_DIM).astype(F32))


def _lane_freqs(dims, theta, width):
    inv = jnp.float32(theta) ** (-jnp.arange(0, dims, 2, dtype=F32) / dims)
    pad = jnp.zeros((width - dims,), F32)
    ones = jnp.ones_like(inv)
    return jnp.concatenate([inv, inv, pad]), jnp.concatenate([-ones, ones, pad])


def rotary_tables(seq_len):
    pos = jnp.arange(seq_len, dtype=F32)[:, None]
    inv_a, sign_a = _lane_freqs(ROPE_DIMS, ROPE_THETA, HEAD_DIM)
    ang_a = pos * inv_a[None, :]
    half = HEAD_DIM // 2
    inv_h, sign_h = _lane_freqs(half, AXIAL_THETA, half)
    zeros = jnp.zeros((half,), F32)
    t = jnp.arange(seq_len, dtype=jnp.int32)
    row_pos = (t // GRID_W).astype(F32)[:, None]
    col_pos = (t % GRID_W).astype(F32)[:, None]
    ang_b = (row_pos * jnp.concatenate([inv_h, zeros])[None, :]
             + col_pos * jnp.concatenate([zeros, inv_h])[None, :])
    sign_b = jnp.concatenate([sign_h, sign_h])
    return (jnp.cos(ang_a), sign_a[None, :] * jnp.sin(ang_a),
            jnp.cos(ang_b), sign_b[None, :] * jnp.sin(ang_b))


DA_QBLK = 128
DA_HALF = 64


DA_ROWS = 256
DA_UNROLL = 16


def _dilated_kernel(q_ref, k_ref, v_ref, c_ref, s_ref, y_ref,
                    qf, kf, vf, qg, kg, vg, qc, kc, vc, oc, lc, o_run, l_run, bias,
                    *, seq_len, scale):
    n = seq_len
    half = ROPE_DIMS // 2
    lane = lax.broadcasted_iota(jnp.int32, (DA_ROWS, HEAD_DIM), 1)
    first = (lane % (2 * half)) < half

    q_i = lax.broadcasted_iota(jnp.int32, (DA_QBLK, 2 * DA_QBLK), 0)
    k_j = lax.broadcasted_iota(jnp.int32, (DA_QBLK, 2 * DA_QBLK), 1)
    for c in range(3):
        bias[c] = jnp.where(jnp.abs(k_j - q_i - c * DA_HALF) <= DA_HALF, 0.0, NEG)

    def rope(c, carry):
        rows = pl.ds(pl.multiple_of(c * DA_ROWS, DA_ROWS), DA_ROWS)
        cos, sin = c_ref[rows, :], s_ref[rows, :]
        for src, dst, mul in ((q_ref, qf, scale), (k_ref, kf, 1.0)):
            x = src[0, rows, :].astype(F32)
            partner = jnp.where(first, pltpu.roll(x, HEAD_DIM - half, 1), pltpu.roll(x, half, 1))
            dst[rows, :] = (x * cos + partner * sin) * mul
        vf[rows, :] = v_ref[0, rows, :].astype(F32)
        return carry

    lax.fori_loop(0, n // DA_ROWS, rope, 0)

    f32_sets = ((qf, kf, vf), (qg, kg, vg))
    assert len(DA_PATTERNS) <= len(f32_sets) + 1
    for pi, (_, dil) in enumerate(DA_PATTERNS):
        la = n // dil
        win = min(2 * DA_QBLK, la)
        per_class = la // DA_QBLK
        sub = min(DA_ROWS, la)

        def natural(r, c0, dil=dil, sub=sub):
            return pl.ds(r + dil * c0, sub, stride=dil) if dil > 1 else pl.ds(c0, sub)

        prev_dil = DA_PATTERNS[pi - 1][1] if pi else 1
        ratio = dil // prev_dil
        sources = f32_sets[max(pi - 1, 0)]
        keep = f32_sets[pi] if 0 < pi < len(DA_PATTERNS) - 1 else (None,) * 3
        for r in range(dil):
            for c0 in range(0, la, sub):
                start = (r % prev_dil) * (n // prev_dil) + r // prev_dil + ratio * c0
                src = pl.ds(start, sub, stride=ratio) if ratio > 1 else pl.ds(start, sub)
                dst = pl.ds(r * la + c0, sub)
                for x_src, x_keep, x_dst in zip(sources, keep, (qc, kc, vc)):
                    x = x_src[src, :]
                    if x_keep is not None:
                        x_keep[dst, :] = x
                    x_dst[dst, :] = x.astype(BF16)

        o_dst, l_dst = (o_run, l_run) if pi == 0 else (oc, lc)

        def band(g, carry, la=la, win=win, per_class=per_class, o_dst=o_dst, l_dst=l_dst):
            base = (g // per_class) * la
            a0 = (g % per_class) * DA_QBLK
            ws = jnp.clip(a0 - DA_HALF, 0, la - win)
            q_rows = pl.ds(pl.multiple_of(base + a0, DA_QBLK), DA_QBLK)
            k_rows = pl.ds(pl.multiple_of(base + ws, DA_HALF), win)
            s = lax.dot_general(qc[q_rows, :], kc[k_rows, :], (((1,), (1,)), ((), ())),
                                preferred_element_type=F32)
            s = s + bias[(a0 - ws) // DA_HALF, :, 0:win]
            m = jnp.max(s, axis=-1, keepdims=True)
            p = jnp.exp(s - m)
            l = jnp.sum(p, axis=-1, keepdims=True)
            o = jnp.dot(p.astype(BF16), vc[k_rows, :], preferred_element_type=F32)
            o_dst[q_rows, :] = o / l
            l_dst[q_rows, :] = jnp.broadcast_to(m + jnp.log(l), (DA_QBLK, HEAD_DIM))
            return carry

        lax.fori_loop(0, n // DA_QBLK, band, 0, unroll=DA_UNROLL)

        if pi == 0:
            continue
        for r in range(dil):
            for c0 in range(0, la, sub):
                nat = natural(r, c0)
                cls = pl.ds(r * la + c0, sub)
                l_old, l_new = l_run[nat, :], lc[cls, :]
                mx = jnp.maximum(l_old, l_new)
                w_old, w_new = jnp.exp(l_old - mx), jnp.exp(l_new - mx)
                tot = w_old + w_new
                o_run[nat, :] = (w_old * o_run[nat, :] + w_new * oc[cls, :]) / tot
                l_run[nat, :] = mx + jnp.log(tot)

    def emit(c, carry):
        rows = pl.ds(pl.multiple_of(c * DA_ROWS, DA_ROWS), DA_ROWS)
        y_ref[0, rows, :] = o_run[rows, :].astype(y_ref.dtype)
        return carry

    lax.fori_loop(0, n // DA_ROWS, emit, 0)


def dilated_attention(proj, cos_t, sin_t, batch, seq_len):
    pv = proj.reshape(batch, seq_len, proj.shape[-1])
    blk = (1, seq_len, HEAD_DIM)

    def col(off):
        return lambda b, h: (b, 0, off // HEAD_DIM + h)

    table = pl.BlockSpec((seq_len, HEAD_DIM), lambda b, h: (0, 0))
    f32_rows = pltpu.VMEM((seq_len, HEAD_DIM), F32)
    bf16_rows = pltpu.VMEM((seq_len, HEAD_DIM), BF16)
    y = pl.pallas_call(
        functools.partial(_dilated_kernel, seq_len=seq_len, scale=HEAD_DIM ** -0.5),
        grid=(batch, DA_HEADS),
        in_specs=[pl.BlockSpec(blk, col(OFF_AQ)), pl.BlockSpec(blk, col(OFF_AK)),
                  pl.BlockSpec(blk, col(OFF_AV)), table, table],
        out_specs=pl.BlockSpec(blk, lambda b, h: (b, 0, h)),
        out_shape=jax.ShapeDtypeStruct((batch, seq_len, GROUP_WIDTH), BF16),
        scratch_shapes=[f32_rows] * 6 + [bf16_rows] * 3 + [f32_rows] * 4
        + [pltpu.VMEM((3, DA_QBLK, 2 * DA_QBLK), F32)],
        compiler_params=_cparams("parallel", "parallel"),
        name="dilated_attention",
    )(pv, pv, pv, cos_t, sin_t)
    return y.reshape(batch * seq_len, GROUP_WIDTH)


GQA_SCORE_ELEMS = 1024 * 1024
LOG2E = 1.4426950408889634


def _gqa_kernel(q_ref, k_ref, v_ref, o_ref):
    k = k_ref[0]
    v = v_ref[0]
    for g in range(GQA_GROUP):
        sl = slice(g * HEAD_DIM, (g + 1) * HEAD_DIM)
        s = lax.dot_general(q_ref[0, :, sl], k, (((1,), (1,)), ((), ())),
                            preferred_element_type=F32)
        m = jnp.max(s, axis=-1, keepdims=True)
        p = jnp.exp2(s - m)
        l = jnp.sum(p, axis=-1, keepdims=True)
        o = jnp.dot(p.astype(BF16), v, preferred_element_type=F32)
        o_ref[0, :, sl] = (o / l).astype(o_ref.dtype)


def gqa_attention(q, k, proj, batch, seq_len):
    tq = min(seq_len, GQA_SCORE_ELEMS // seq_len)
    qv = q.reshape(batch, seq_len, GQA_HEADS * HEAD_DIM)
    kv = k.reshape(batch, seq_len, GQA_KV_HEADS * HEAD_DIM)
    pv = proj.reshape(batch, seq_len, proj.shape[-1])
    gw = GQA_GROUP * HEAD_DIM
    v_blk0 = OFF_BV // HEAD_DIM
    y = pl.pallas_call(
        _gqa_kernel,
        grid=(batch, GQA_KV_HEADS, seq_len // tq),
        in_specs=[pl.BlockSpec((1, tq, gw), lambda b, h, i: (b, i, h)),
                  pl.BlockSpec((1, seq_len, HEAD_DIM), lambda b, h, i: (b, 0, h)),
                  pl.BlockSpec((1, seq_len, HEAD_DIM), lambda b, h, i: (b, 0, v_blk0 + h))],
        out_specs=pl.BlockSpec((1, tq, gw), lambda b, h, i: (b, i, h)),
        out_shape=jax.ShapeDtypeStruct((batch, seq_len, GQA_HEADS * HEAD_DIM), BF16),
        compiler_params=_cparams("parallel", "parallel", "arbitrary"),
        name="gqa_attention",
    )(qv, kv, pv)
    return y.reshape(batch * seq_len, GQA_HEADS * HEAD_DIM)


DFT_TM = 512
DFT_TN = 512
DFT_COLS = 256


HY_TC = 512


def _short_conv_kernel(u_ref, w_ref, b_ref, oe_ref, oo_ref, uf_ref):
    n = u_ref.shape[1]
    hn = n // 2
    row = lax.broadcasted_iota(jnp.int32, (hn, LANES), 0)
    for c in range(HY_TC // LANES):
        cs = slice(c * LANES, (c + 1) * LANES)
        uf_ref[c] = u_ref[0, :, cs].astype(F32)
        ue = uf_ref[c, pl.ds(0, hn, stride=2), :]
        uo = uf_ref[c, pl.ds(1, hn, stride=2), :]
        uo_prev = jnp.where(row == 0, 0.0, pltpu.roll(uo, 1, 0))
        ue_next = jnp.where(row == hn - 1, 0.0, pltpu.roll(ue, hn - 1, 0))
        w0, w1, w2, b = w_ref[0:1, cs], w_ref[1:2, cs], w_ref[2:3, cs], b_ref[:, cs]
        oe_ref[0, :, cs] = b + uo_prev * w0 + ue * w1 + uo * w2
        oo_ref[0, :, cs] = b + ue * w0 + uo * w1 + ue_next * w2


def short_conv(proj, conv_w, conv_b, batch, seq_len):
    pv = proj.reshape(batch, seq_len, proj.shape[-1])
    width = conv_w.shape[-1]
    blk0 = OFF_CU // HY_TC
    out = jax.ShapeDtypeStruct((batch, seq_len // 2, width), F32)
    out_spec = pl.BlockSpec((1, seq_len // 2, HY_TC), lambda b, j: (b, 0, j))
    return pl.pallas_call(
        _short_conv_kernel,
        grid=(batch, width // HY_TC),
        in_specs=[pl.BlockSpec((1, seq_len, HY_TC), lambda b, j: (b, 0, blk0 + j)),
                  pl.BlockSpec((3, HY_TC), lambda b, j: (0, j)),
                  pl.BlockSpec((1, HY_TC), lambda b, j: (0, j))],
        out_specs=[out_spec, out_spec],
        out_shape=[out, out],
        scratch_shapes=[pltpu.VMEM((HY_TC // LANES, seq_len, LANES), F32)],
        compiler_params=_cparams("parallel", "parallel"),
        name="hyena_short_conv",
    )(pv, conv_w.astype(F32), conv_b.reshape(1, width).astype(F32))


def _interleave_kernel(e_ref, o_ref, y_ref, buf_ref):
    hn = e_ref.shape[1]
    for c in range(HY_TC // LANES):
        cs = slice(c * LANES, (c + 1) * LANES)
        buf_ref[c, pl.ds(0, hn, stride=2), :] = e_ref[0, :, cs]
        buf_ref[c, pl.ds(1, hn, stride=2), :] = o_ref[0, :, cs]
        y_ref[0, :, cs] = buf_ref[c].astype(y_ref.dtype)


def interleave(even, odd, out_dtype):
    batch, hn, width = even.shape
    in_spec = pl.BlockSpec((1, hn, HY_TC), lambda b, j: (b, 0, j))
    return pl.pallas_call(
        _interleave_kernel,
        grid=(batch, width // HY_TC),
        in_specs=[in_spec, in_spec],
        out_specs=pl.BlockSpec((1, 2 * hn, HY_TC), lambda b, j: (b, 0, j)),
        out_shape=jax.ShapeDtypeStruct((batch, 2 * hn, width), out_dtype),
        scratch_shapes=[pltpu.VMEM((HY_TC // LANES, 2 * hn, LANES), F32)],
        compiler_params=_cparams("parallel", "parallel"),
        name="hyena_interleave",
    )(even, odd)


def dft_matrices(seq_len):
    n = seq_len
    assert n % 4 == 0
    half = DFT_TM // 2
    i = jnp.arange(n, dtype=jnp.int32)
    tile, w = i // DFT_TM, i % DFT_TM
    imag = (w // half) == 1
    odd = 2 * (tile * half + w % half) + 1
    unit = math.pi / (2 * n)
    fwd = []
    for parity in range(2):
        s = jnp.arange(parity, n, 2, dtype=jnp.int32)
        ang = ((odd[:, None] * s[None, :]) % (4 * n)).astype(F32) * unit
        fwd.append(jnp.where(imag[:, None], -jnp.sin(ang), jnp.cos(ang)).astype(BF16))
    fwd = lax.optimization_barrier(fwd)
    f = jnp.arange(n // 2, dtype=jnp.int32)
    fold = []
    for eighths in ((2 * f + 1) % 8, (2 * (n - 1 - f) + 1) % 8):
        ang = eighths.astype(F32) * (math.pi / 4)
        fold += [(jnp.cos(ang) / n)[:, None], (jnp.sin(ang) / n)[:, None]]
    return fwd[0], fwd[1], fwd[0].T, fwd[1].T, tuple(fold)


def hyena_filters(seq_len, parity, w1, b1, fr1, w2, b2, fr2, w3, decay):
    hp = lax.Precision.HIGHEST
    n = jnp.arange(parity, seq_len, 2, dtype=F32)
    t = n / (seq_len - 1)
    f = jnp.linspace(1e-4, HY_BANDS - 1, HY_BANDS, dtype=F32)
    ang = (2.0 * math.pi / seq_len) * n[None, :] * f[:, None]
    z = jnp.concatenate([t[None, :], jnp.cos(ang), -jnp.sin(ang)], axis=0)
    h = jnp.sin(fr1[:, None] * (jnp.dot(w1.T, z, precision=hp) + b1[:, None]))
    h = jnp.sin(fr2[:, None] * (jnp.dot(w2.T, h, precision=hp) + b2[:, None]))
    h = lax.dot_general(h, w3, (((0,), (0,)), ((), ())), precision=hp)
    r = jnp.abs(n - seq_len // 2) / (seq_len // 2)
    return h * jnp.exp(-r[:, None] * decay[None, :])


def _cmul(ar, ai, br, bi):
    return ar * br - ai * bi, ar * bi + ai * br


def _dft_fwd_kernel(*refs, with_filter):
    if with_filter:
        fe_ref, fo_ref, ze_ref, zo_ref, h_ref, o_ref, zbe_ref, zbo_ref = refs
    else:
        fe_ref, fo_ref, ze_ref, zo_ref, c1r, c1i, c2r, c2i, o_ref, zbe_ref, zbo_ref = refs
    tn = ze_ref.shape[2]
    half = DFT_TM // 2

    @pl.when(pl.program_id(2) == 0)
    def _():
        zbe_ref[...] = ze_ref[0].astype(BF16)
        zbo_ref[...] = zo_ref[0].astype(BF16)

    for c0 in range(0, tn, DFT_COLS):
        cs = slice(c0, c0 + DFT_COLS)
        qo = slice(tn + c0, tn + c0 + DFT_COLS)
        a = jnp.dot(fe_ref[...], zbe_ref[:, cs], preferred_element_type=F32)
        b = jnp.dot(fo_ref[...], zbo_ref[:, cs], preferred_element_type=F32)
        u1r, u1i = a[:half] + b[:half], a[half:] + b[half:]
        u2r, u2i = a[:half] - b[:half], b[half:] - a[half:]
        if not with_filter:
            u1r, u1i = _cmul(u1r, u1i, c1r[...], c1i[...])
            u2r, u2i = _cmul(u2r, u2i, c2r[...], c2i[...])
            for q, val in enumerate((u1r, u1i, u2r, u2i)):
                o_ref[0, q * half:(q + 1) * half, cs] = val
            continue
        p1r, p1i = _cmul(u1r, u1i, h_ref[0:half, cs], h_ref[half:2 * half, cs])
        p2r, p2i = _cmul(u2r, u2i, h_ref[2 * half:3 * half, cs], h_ref[3 * half:, cs])
        o_ref[0, :half, cs] = (p1r + p2r).astype(o_ref.dtype)
        o_ref[0, half:, cs] = (p1i - p2i).astype(o_ref.dtype)
        o_ref[0, :half, qo] = (p1r - p2r).astype(o_ref.dtype)
        o_ref[0, half:, qo] = (p1i + p2i).astype(o_ref.dtype)


def dft_forward(tables, z_pair, z_blk0=0, spec=None, spec_blk0=0):
    batch, hn, _ = z_pair[0].shape
    seq_len = 2 * hn
    n_ch = HY_CH if spec is not None else z_pair[0].shape[2]
    f_spec = pl.BlockSpec((DFT_TM, hn), lambda b, j, i: (i, 0))
    z_spec = pl.BlockSpec((1, hn, DFT_TN), lambda b, j, i: (b, 0, z_blk0 + j))
    in_specs = [f_spec, f_spec, z_spec, z_spec]
    args = [tables[0], tables[1], z_pair[0], z_pair[1]]
    if spec is not None:
        in_specs.append(pl.BlockSpec((2 * DFT_TM, DFT_TN), lambda b, j, i: (i, spec_blk0 + j)))
        args.append(spec)
        out_spec = pl.BlockSpec((1, DFT_TM, 2 * DFT_TN), lambda b, j, i: (b, i, j))
        out_shape = jax.ShapeDtypeStruct((batch, seq_len, 2 * n_ch), BF16)
    else:
        in_specs += [pl.BlockSpec((DFT_TM // 2, 1), lambda b, j, i: (i, 0))] * 4
        args += list(tables[4])
        out_spec = pl.BlockSpec((1, 2 * DFT_TM, DFT_TN), lambda b, j, i: (b, i, j))
        out_shape = jax.ShapeDtypeStruct((batch, 2 * seq_len, n_ch), F32)
    return pl.pallas_call(
        functools.partial(_dft_fwd_kernel, with_filter=spec is not None),
        grid=(batch, n_ch // DFT_TN, seq_len // DFT_TM),
        in_specs=in_specs,
        out_specs=out_spec,
        out_shape=out_shape,
        scratch_shapes=[pltpu.VMEM((hn, DFT_TN), BF16), pltpu.VMEM((hn, DFT_TN), BF16)],
        compiler_params=_cparams("parallel", "parallel", "arbitrary"),
        name="hyena_dft_fwd" if spec is not None else "hyena_filter_dft",
    )(*args)


def _dft_inv_kernel(ge_ref, go_ref, q_ref, ze_ref, zo_ref, xe_ref, xo_ref, sk_ref,
                    oe_ref, oo_ref):
    tn = ze_ref.shape[2]
    sk = sk_ref[...]
    ye = jnp.dot(ge_ref[...], q_ref[0, :, :tn], preferred_element_type=F32)
    oe_ref[0] = xe_ref[0] * (ye + sk * ze_ref[0])
    yo = jnp.dot(go_ref[...], q_ref[0, :, tn:], preferred_element_type=F32)
    oo_ref[0] = xo_ref[0] * (yo + sk * zo_ref[0])


def dft_inverse(tables, q, z_pair, z_blk0, gate_pair, gate_blk0, skip, skip_blk0, tm=256):
    batch, seq_len, _ = q.shape
    hn = seq_len // 2
    g_spec = pl.BlockSpec((tm, seq_len), lambda b, j, i: (i, 0))
    z_spec = pl.BlockSpec((1, tm, DFT_TN), lambda b, j, i: (b, i, z_blk0 + j))
    x_spec = pl.BlockSpec((1, tm, DFT_TN), lambda b, j, i: (b, i, gate_blk0 + j))
    out_spec = pl.BlockSpec((1, tm, DFT_TN), lambda b, j, i: (b, i, j))
    out = jax.ShapeDtypeStruct((batch, hn, HY_CH), F32)
    return pl.pallas_call(
        _dft_inv_kernel,
        grid=(batch, HY_CH // DFT_TN, hn // tm),
        in_specs=[g_spec, g_spec,
                  pl.BlockSpec((1, seq_len, 2 * DFT_TN), lambda b, j, i: (b, 0, j)),
                  z_spec, z_spec, x_spec, x_spec,
                  pl.BlockSpec((1, DFT_TN), lambda b, j, i: (0, skip_blk0 + j))],
        out_specs=[out_spec, out_spec],
        out_shape=[out, out],
        compiler_params=_cparams("parallel", "parallel", "arbitrary"),
        name="hyena_dft_inv",
    )(tables[2], tables[3], q, z_pair[0], z_pair[1], gate_pair[0], gate_pair[1], skip)


def hyena_mixer(proj, batch, seq_len, tables, spec, conv_w, conv_b, skip):
    uc = short_conv(proj, conv_w, conv_b, batch, seq_len)
    nb = HY_CH // DFT_TN
    sk = skip.reshape(1, HY_ORDER * HY_CH).astype(F32)
    q = dft_forward(tables, uc, 0, spec, 0)
    z = dft_inverse(tables, q, uc, 0, uc, nb, sk, 0)
    q = dft_forward(tables, z, 0, spec, nb)
    y = dft_inverse(tables, q, z, 0, uc, 2 * nb, sk, nb)
    return interleave(y[0], y[1], BF16).reshape(batch * seq_len, HY_CH)


ML_CHUNK = 256
ML_GROWS = 8


def _split3_dot(x, tri):
    hi = x.astype(BF16)
    r1 = x - hi.astype(F32)
    mid = r1.astype(BF16)
    lo = (r1 - mid.astype(F32)).astype(BF16)
    return (jnp.dot(hi, tri, preferred_element_type=F32)
            + jnp.dot(mid, tri, preferred_element_type=F32)
            + jnp.dot(lo, tri, preferred_element_type=F32))


def _log_sigmoid(x):
    return jnp.minimum(x, 0.0) - jnp.log1p(jnp.exp(-jnp.abs(x)))


def _mlstm_kernel(q_ref, k_ref, v_ref, o_ref, g_ref, gb_ref, ng_ref, y_ref,
                  hf_ref, hb_ref, cf_ref, nf_ref, mf_ref, cb_ref, nb_ref, mb_ref, *, seq_len):
    lc = ML_CHUNK
    nc = seq_len // lc
    for ref in (cf_ref, nf_ref, mf_ref, cb_ref, nb_ref, mb_ref):
        ref[...] = jnp.zeros(ref.shape, ref.dtype)

    t_i = lax.broadcasted_iota(jnp.int32, (lc, lc), 0)
    s_i = lax.broadcasted_iota(jnp.int32, (lc, lc), 1)
    eye = t_i == s_i
    lower = s_i <= t_i
    upper = s_i >= t_i
    lower_b = lower.astype(BF16)
    upper_b = upper.astype(BF16)
    k_scale = ML_HEAD_DIM ** -0.5

    def to_col(row):
        return jnp.sum(jnp.where(eye, row, 0.0), axis=1, keepdims=True)

    def chunk(c, reverse, c_ref, n_ref, m_ref, h_ref):
        t0 = pl.multiple_of(c * lc, lc)
        q = q_ref[0, pl.ds(t0, lc), :]
        k = k_ref[0, pl.ds(t0, lc), :] * k_scale
        v = v_ref[0, pl.ds(t0, lc), :]
        g = g_ref[:, pl.ds(t0, lc)] + gb_ref[0]
        r0 = 2 if reverse else 0
        li_row = g[r0:r0 + 1, :]
        lf_row = _log_sigmoid(g[r0 + 1:r0 + 2, :])
        b_row = _split3_dot(jnp.broadcast_to(lf_row, (ML_GROWS, lc)),
                            lower_b if reverse else upper_b)[0:1, :]
        seen = upper if reverse else lower
        b_col = to_col(b_row)
        li_col = to_col(li_row)
        m_old = m_ref[...]
        dm = jnp.where(seen, b_col - b_row + li_row, NEG)
        inter = b_col + m_old
        m_t = jnp.maximum(inter, jnp.max(dm, axis=1, keepdims=True))
        wm = jnp.exp(dm - m_t)
        a_inter = jnp.exp(inter - m_t)
        qk = lax.dot_general(q, k, (((1,), (1,)), ((), ())), preferred_element_type=F32)
        s = qk * wm
        c_old = c_ref[...]
        n_old = n_ref[...]
        num = (a_inter * jnp.dot(q, c_old.astype(BF16), preferred_element_type=F32)
               + jnp.dot(s.astype(BF16), v, preferred_element_type=F32))
        qn = jnp.sum(q.astype(F32) * n_old, axis=1, keepdims=True)
        den = a_inter * qn + jnp.sum(s, axis=1, keepdims=True)
        h_ref[pl.ds(t0, lc), :] = num / jnp.maximum(jnp.abs(den), jnp.exp(-m_t))
        b_last = b_row[:, 0:1] if reverse else b_row[:, lc - 1:lc]
        g_row = b_last - b_row + li_row
        g_col = b_last - b_col + li_col
        m_new = jnp.maximum(b_last + m_old, jnp.max(g_row, axis=1, keepdims=True))
        a_old = jnp.exp(b_last + m_old - m_new)
        kw = k.astype(F32) * jnp.exp(g_col - m_new)
        c_ref[...] = a_old * c_old + lax.dot_general(
            kw.astype(BF16), v, (((0,), (0,)), ((), ())), preferred_element_type=F32)
        n_ref[...] = a_old * n_old + jnp.sum(kw, axis=0, keepdims=True)
        m_ref[...] = m_new

    def step(c, carry):
        chunk(c, False, cf_ref, nf_ref, mf_ref, hf_ref)
        chunk(nc - 1 - c, True, cb_ref, nb_ref, mb_ref, hb_ref)
        return carry

    lax.fori_loop(0, nc, step, 0)

    def finish(c, carry):
        t0 = pl.multiple_of(c * lc, lc)
        hs = hf_ref[pl.ds(t0, lc), :] + hb_ref[pl.ds(t0, lc), :]
        ms = jnp.mean(hs * hs, axis=-1, keepdims=True)
        hs = hs * lax.rsqrt(ms + EPS) * ng_ref[...]
        gate = jax.nn.sigmoid(o_ref[0, pl.ds(t0, lc), :].astype(F32))
        y_ref[0, pl.ds(t0, lc), :] = (hs * gate).astype(y_ref.dtype)
        return carry

    lax.fori_loop(0, nc, finish, 0)


def _gate_proj_kernel(w_ref, h_ref, o_ref):
    o_ref[...] = lax.dot_general(w_ref[...], h_ref[...], (((1,), (1,)), ((), ())),
                                 preferred_element_type=F32)


def gate_rows_weight(w_gate_cols):
    d = w_gate_cols.shape[0]
    w = w_gate_cols.T.reshape(4, ML_HEADS, d).transpose(1, 0, 2)
    w = jnp.pad(w, ((0, 0), (0, ML_GROWS - 4), (0, 0)))
    return w.reshape(ML_HEADS * ML_GROWS, d).astype(BF16)


def gate_projection(h, w_rows, tm=1024):
    m, d = h.shape
    rows = w_rows.shape[0]
    return pl.pallas_call(
        _gate_proj_kernel,
        grid=(m // tm,),
        in_specs=[pl.BlockSpec((rows, d), lambda i: (0, 0)),
                  pl.BlockSpec((tm, d), lambda i: (i, 0))],
        out_specs=pl.BlockSpec((rows, tm), lambda i: (0, i)),
        out_shape=jax.ShapeDtypeStruct((rows, m), F32),
        compiler_params=_cparams("parallel"),
        name="in_proj_gates",
    )(w_rows, h)


def mlstm_mixer(proj, gates_t, gate_b, norm_g, batch, seq_len):
    d = ML_HEAD_DIM
    pv = proj.reshape(batch, seq_len, proj.shape[-1])
    gb = jnp.pad(gate_b.astype(F32).T, ((0, 0), (0, ML_GROWS - 4)))[..., None]
    blk = (1, seq_len, d)

    def col(off):
        return lambda b, h: (b, 0, off // d + h)

    y = pl.pallas_call(
        functools.partial(_mlstm_kernel, seq_len=seq_len),
        grid=(batch, ML_HEADS),
        in_specs=[pl.BlockSpec(blk, col(OFF_DQ)),
                  pl.BlockSpec(blk, col(OFF_DK)),
                  pl.BlockSpec(blk, col(OFF_DV)),
                  pl.BlockSpec(blk, col(OFF_DO)),
                  pl.BlockSpec((ML_GROWS, seq_len), lambda b, h: (h, b)),
                  pl.BlockSpec((1, ML_GROWS, 1), lambda b, h: (h, 0, 0)),
                  pl.BlockSpec((1, d), lambda b, h: (0, h))],
        out_specs=pl.BlockSpec(blk, lambda b, h: (b, 0, h)),
        out_shape=jax.ShapeDtypeStruct((batch, seq_len, ML_HEADS * d), BF16),
        scratch_shapes=[pltpu.VMEM((seq_len, d), F32), pltpu.VMEM((seq_len, d), F32),
                        pltpu.VMEM((d, d), F32), pltpu.VMEM((1, d), F32), pltpu.VMEM((1, 1), F32),
                        pltpu.VMEM((d, d), F32), pltpu.VMEM((1, d), F32), pltpu.VMEM((1, 1), F32)],
        compiler_params=_cparams("parallel", "parallel"),
        name="mlstm",
    )(pv, pv, pv, pv, gates_t, gb, norm_g.reshape(1, ML_HEADS * d).astype(F32))
    return y.reshape(batch * seq_len, ML_HEADS * d)


def _trunk(x, wts, shared):
    batch, seq_len, d = x.shape
    m = batch * seq_len
    ca, sa, cb, sb = shared[:4]
    dft = shared[4:]
    q_scale = HEAD_DIM ** -0.5
    x2 = x.reshape(m, d)
    for l in range(DEPTH):
        w = wts[l]
        h = rmsnorm(x2, w["norm1_g"], BF16)
        proj = linear(h, w["w_in_main"], BF16, 2048, 512, name="in_proj")
        gates = gate_projection(h, w["w_in_gate"])
        y_a = dilated_attention(proj, ca, sa, batch, seq_len)
        b_q = rope_prep(proj, OFF_BQ, GQA_HEADS, cb, sb, w["qk_norm_g"][0], seq_len,
                        half=HEAD_DIM // 4, norm=True, scale=q_scale * LOG2E)
        b_k = rope_prep(proj, OFF_BK, GQA_KV_HEADS, cb, sb, w["qk_norm_g"][1], seq_len,
                        half=HEAD_DIM // 4, norm=True, scale=1.0)
        y_b = gqa_attention(b_q, b_k, proj, batch, seq_len)
        filt = [hyena_filters(seq_len, parity, w["hy_w1"], w["hy_b1"], w["hy_freq1"], w["hy_w2"],
                              w["hy_b2"], w["hy_freq2"], w["hy_w3"], w["hy_decay"])[None]
                for parity in range(2)]
        spec = dft_forward(dft, filt)[0]
        y_c = hyena_mixer(proj, batch, seq_len, dft, spec,
                          w["hy_conv_w"], w["hy_conv_b"], w["hy_skip"])
        y_d = mlstm_mixer(proj, gates, w["ml_gate_b"], w["ml_norm_g"], batch, seq_len)
        y = jnp.concatenate([y_a, y_b, y_c, y_d], axis=-1)
        x2 = linear(y, w["w_out"], F32, 1024, 512, residual=x2, name="out_proj")
        h = rmsnorm(x2, w["norm2_g"], BF16)
        act = ffn_up(h, w["w_gate"], w["w_up"], 2048, 256)
        x2 = linear(act, w["w_down"], F32, 512, 512, residual=x2, name="ffn_down")
    return rmsnorm(x2, shared_final_g(wts), F32).reshape(batch, seq_len, d)


def shared_final_g(wts):
    return wts[0]["final_g"]


def kernel(x_prompt, x_sample, norm1_g, w_in, qk_norm_g, hy_conv_w, hy_conv_b, hy_w1, hy_b1,
           hy_freq1, hy_w2, hy_b2, hy_freq2, hy_w3, hy_decay, hy_skip, ml_gate_b, ml_norm_g,
           w_out, norm2_g, w_gate, w_up, w_down, final_g):
    wts = []
    d_rows = D_MODEL
    for l in range(DEPTH):
        wts.append(dict(
            norm1_g=norm1_g[l], w_in_main=w_in[l, :, :N_MAIN].astype(BF16),
            w_in_gate=gate_rows_weight(w_in[l, :, N_MAIN:]), qk_norm_g=qk_norm_g[l],
            hy_conv_w=hy_conv_w[l], hy_conv_b=hy_conv_b[l], hy_w1=hy_w1[l], hy_b1=hy_b1[l],
            hy_freq1=hy_freq1[l], hy_w2=hy_w2[l], hy_b2=hy_b2[l], hy_freq2=hy_freq2[l],
            hy_w3=hy_w3[l], hy_decay=hy_decay[l], hy_skip=hy_skip[l], ml_gate_b=ml_gate_b[l],
            ml_norm_g=ml_norm_g[l], w_out=cast_layer_bf16(w_out, l, D_MODEL, d_rows, 512),
            norm2_g=norm2_g[l],
            w_gate=cast_layer_bf16(w_gate, l, FF_DIM, d_rows, 256),
            w_up=cast_layer_bf16(w_up, l, FF_DIM, d_rows, 256),
            w_down=cast_layer_bf16(w_down, l, D_MODEL, FF_DIM // 4, 1024), final_g=final_g))
    outs = []
    for x in (x_prompt, x_sample):
        seq_len = x.shape[1]
        shared = rotary_tables(seq_len) + dft_matrices(seq_len)
        outs.append(_trunk(x, wts, shared))
    return tuple(outs)
```
